```python
import math, functools
import jax, jax.numpy as jnp
from jax import lax
import numpy as np

D_MODEL = 1024
BATCH = 2
SEQ = 8192
DEPTH = 2
DEC_BATCH = 32
DEC_SEQ = 1
PAST_LEN = 16384
PAGE_SIZE = 128

EPS = 1e-6
N_BRANCHES = 3
S5_GROUP = 16
S5_WIDTH = D_MODEL // 2
S5_GROUPS = S5_WIDTH // S5_GROUP
S5_STATE = 64
N_HEADS = 8
N_KV_HEADS = 4
HEAD_DIM = 64
ATT_WIDTH = N_HEADS * HEAD_DIM
KV_WIDTH = N_KV_HEADS * HEAD_DIM
ROT_DIM = HEAD_DIM // 4
ROPE_THETA = 500000.0
IDX_HEADS = 4
IDX_DIM = 64
IDX_SCALE = (IDX_HEADS * IDX_DIM) ** -0.5
TOPK_MAX = 256
Q_BLOCK = 128
SSM_INNER = D_MODEL
SSM_HEAD_DIM = 64
SSM_HEADS = SSM_INNER // SSM_HEAD_DIM
SSM_GROUPS = 2
SSM_STATE = 128
CONV_WIDTH = 4
CONV_DIM = SSM_INNER + 2 * SSM_GROUPS * SSM_STATE
SSD_CHUNK = 128
D_FF = 4 * D_MODEL
IN_SIZES = (S5_WIDTH, ATT_WIDTH, KV_WIDTH, KV_WIDTH, IDX_HEADS * IDX_DIM, IDX_DIM, IDX_HEADS,
            SSM_INNER, CONV_DIM, SSM_HEADS, N_BRANCHES * D_MODEL)
IN_DIM = sum(IN_SIZES)

kernel_name = "hybrid_s5_dsa_ssd_gated_decoder_step"


def rms_norm(x, g):
    xf = x.astype(jnp.float32)
    y = xf * lax.rsqrt(jnp.mean(xf * xf, axis=-1, keepdims=True) + EPS)
    return (y * g.astype(jnp.float32)).astype(x.dtype)


def rotary(x, pos):
    half = ROT_DIM // 2
    inv_freq = ROPE_THETA ** (-jnp.arange(half, dtype=jnp.float32) / half)
    ang = pos.astype(jnp.float32)[:, None] * inv_freq[None, :]
    cos = jnp.cos(ang)[:, None, :]
    sin = jnp.sin(ang)[:, None, :]
    xr = x[..., :ROT_DIM].astype(jnp.float32)
    x1, x2 = xr[..., :half], xr[..., half:]
    rot = jnp.concatenate([x1 * cos - x2 * sin, x2 * cos + x1 * sin], axis=-1)
    return jnp.concatenate([rot.astype(x.dtype), x[..., ROT_DIM:]], axis=-1)


def split_cols(h):
    outs = []
    start = 0
    for size in IN_SIZES:
        outs.append(h[..., start:start + size])
        start += size
    return outs


def sparse_attend(q, k_sel, v_sel, valid):
    bsz, nq = q.shape[0], q.shape[1]
    qg = q.reshape(bsz, nq, N_KV_HEADS, N_HEADS // N_KV_HEADS, HEAD_DIM).astype(jnp.float32)
    s = jnp.einsum("bqgrd,bqkgd->bqgrk", qg, k_sel.astype(jnp.float32)) * (HEAD_DIM ** -0.5)
    s = jnp.where(valid[:, :, None, None, :], s, -jnp.inf)
    p = jax.nn.softmax(s, axis=-1)
    o = jnp.einsum("bqgrk,bqkgd->bqgrd", p, v_sel.astype(jnp.float32))
    return o.reshape(bsz, nq, ATT_WIDTH).astype(q.dtype)


def dsa_prompt(q, k, v, iq, ik, iw):
    bsz, seq = q.shape[0], q.shape[1]
    n_sel = min(TOPK_MAX, seq // 4)
    n_blk = seq // Q_BLOCK
    key_pos = jnp.arange(seq)
    ikf = ik.astype(jnp.float32)

    def to_blocks(a):
        return jnp.moveaxis(a.reshape((bsz, n_blk, Q_BLOCK) + a.shape[2:]), 1, 0)

    def block(args):
        qb, iqb, iwb, qpos = args
        dots = jnp.einsum("bqhd,bsd->bqhs", iqb.astype(jnp.float32), ikf)
        score = jnp.einsum("bqh,bqhs->bqs", iwb.astype(jnp.float32), jax.nn.relu(dots)) * IDX_SCALE
        admissible = key_pos[None, :] <= qpos[:, None]
        score = jnp.where(admissible[None], score, -jnp.inf)
        _, sel = lax.top_k(score, n_sel)
        k_sel = jax.vmap(lambda kb, ib: kb[ib])(k, sel)
        v_sel = jax.vmap(lambda vb, ib: vb[ib])(v, sel)
        valid = sel <= qpos[None, :, None]
        return sparse_attend(qb, k_sel, v_sel, valid)

    out = lax.map(block, (to_blocks(q), to_blocks(iq), to_blocks(iw), key_pos.reshape(n_blk, Q_BLOCK)))
    return jnp.moveaxis(out, 0, 1).reshape(bsz, seq, ATT_WIDTH)


def dsa_sample(q, k_new, v_new, iq, ik_new, iw, cache_k, cache_v, cache_ik, page_table):
    bsz, n_new = q.shape[0], q.shape[1]
    past = page_table.shape[1] * PAGE_SIZE
    total = past + n_new
    n_sel = min(TOPK_MAX, total // 4)
    ik_past = cache_ik[page_table].reshape(bsz, past, IDX_DIM)
    ik_all = jnp.concatenate([ik_past.astype(jnp.float32), ik_new.astype(jnp.float32)], axis=1)
    dots = jnp.einsum("bqhd,bsd->bqhs", iq.astype(jnp.float32), ik_all)
    score = jnp.einsum("bqh,bqhs->bqs", iw.astype(jnp.float32), jax.nn.relu(dots)) * IDX_SCALE
    q_pos = past + jnp.arange(n_new)
    admissible = jnp.arange(total)[None, :] <= q_pos[:, None]
    score = jnp.where(admissible[None], score, -jnp.inf)
    _, sel = lax.top_k(score, n_sel)
    bidx = jnp.arange(bsz)[:, None, None]
    sel_past = jnp.minimum(sel, past - 1)
    phys = page_table[bidx, sel_past // PAGE_SIZE]
    off = sel_past % PAGE_SIZE
    sel_cur = jnp.clip(sel - past, 0, n_new - 1)
    in_past = (sel < past)[..., None, None]
    k_sel = jnp.where(in_past, cache_k[phys, off].astype(k_new.dtype), k_new[bidx, sel_cur])
    v_sel = jnp.where(in_past, cache_v[phys, off].astype(v_new.dtype), v_new[bidx, sel_cur])
    valid = sel <= q_pos[None, :, None]
    return sparse_attend(q, k_sel, v_sel, valid)


def s5_mixer(u, h0_re, h0_im, a_re, a_im, log_dt, b_re, b_im, c_re, c_im, d_skip, w_glu):
    bsz, seq = u.shape[0], u.shape[1]
    uf = u.astype(jnp.float32).reshape(bsz, seq, S5_GROUPS, S5_GROUP)
    ar, ai = a_re.astype(jnp.float32), a_im.astype(jnp.float32)
    dt = jnp.exp(log_dt.astype(jnp.float32))[:, None]
    mag = jnp.exp(dt * ar)
    abar_re, abar_im = mag * jnp.cos(dt * ai), mag * jnp.sin(dt * ai)
    den = ar * ar + ai * ai
    nr, ni = abar_re - 1.0, abar_im
    coef_re = (nr * ar + ni * ai) / den
    coef_im = (ni * ar - nr * ai) / den
    bu_re = jnp.einsum("blgc,gnc->blgn", uf, b_re.astype(jnp.float32))
    bu_im = jnp.einsum("blgc,gnc->blgn", uf, b_im.astype(jnp.float32))
    in_re = coef_re * bu_re - coef_im * bu_im
    in_im = coef_re * bu_im + coef_im * bu_re
    h0r, h0i = h0_re.astype(jnp.float32), h0_im.astype(jnp.float32)
    in_re = in_re.at[:, 0].add(abar_re * h0r - abar_im * h0i)
    in_im = in_im.at[:, 0].add(abar_re * h0i + abar_im * h0r)
    a_seq_re = jnp.broadcast_to(abar_re, in_re.shape)
    a_seq_im = jnp.broadcast_to(abar_im, in_im.shape)

    def combine(e1, e2):
        a1r, a1i, b1r, b1i = e1
        a2r, a2i, b2r, b2i = e2
        return (a2r * a1r - a2i * a1i, a2r * a1i + a2i * a1r,
                a2r * b1r - a2i * b1i + b2r, a2r * b1i + a2i * b1r + b2i)

    _, _, hr, hi = lax.associative_scan(combine, (a_seq_re, a_seq_im, in_re, in_im), axis=1)
    y = (jnp.einsum("blgn,gcn->blgc", hr, c_re.astype(jnp.float32))
         - jnp.einsum("blgn,gcn->blgc", hi, c_im.astype(jnp.float32)))
    y = y.reshape(bsz, seq, S5_WIDTH) + d_skip.astype(jnp.float32) * uf.reshape(bsz, seq, S5_WIDTH)
    g = jax.nn.gelu(y)
    out = g * jax.nn.sigmoid(g @ w_glu.astype(jnp.float32))
    return out.astype(u.dtype), hr[:, -1], hi[:, -1]


def ssd_chunked(xh, dt, a, bm, cm, h0):
    bsz, seq = xh.shape[0], xh.shape[1]
    q = min(SSD_CHUNK, seq)
    n_chunks = -(-seq // q)
    pad = n_chunks * q - seq

    def chunked(t):
        t = jnp.pad(t, [(0, 0), (0, pad)] + [(0, 0)] * (t.ndim - 2))
        return t.reshape((bsz, n_chunks, q) + t.shape[2:])

    r = SSM_HEADS // SSM_GROUPS
    x = chunked(xh).reshape(bsz, n_chunks, q, SSM_GROUPS, r, SSM_HEAD_DIM)
    dtc = chunked(dt).reshape(bsz, n_chunks, q, SSM_GROUPS, r)
    b_c = chunked(bm)
    c_c = chunked(cm)
    cs = jnp.cumsum(dtc * a.reshape(SSM_GROUPS, r), axis=2)
    causal = jnp.tril(jnp.ones((q, q), dtype=bool))[:, :, None, None]
    seg = cs[:, :, :, None] - cs[:, :, None, :]
    decay_in = jnp.exp(jnp.where(causal, seg, -jnp.inf))
    cb = jnp.einsum("bcign,bcjgn->bcijg", c_c, b_c)
    w = cb[..., None] * decay_in * dtc[:, :, None]
    y_diag = jnp.einsum("bcijgr,bcjgrp->bcigrp", w, x)
    decay_out = jnp.exp(cs[:, :, -1:] - cs)
    states = jnp.einsum("bcjgn,bcjgr,bcjgrp->bcgrpn", b_c, decay_out * dtc, x)
    chunk_decay = jnp.exp(cs[:, :, -1])

    def step(h, inp):
        st, dec = inp
        return dec[..., None, None] * h + st, h

    h_init = h0.reshape(bsz, SSM_GROUPS, r, SSM_HEAD_DIM, SSM_STATE)
    h_final, h_prev = lax.scan(step, h_init, (jnp.moveaxis(states, 1, 0), jnp.moveaxis(chunk_decay, 1, 0)))
    h_prev = jnp.moveaxis(h_prev, 0, 1)
    y_off = jnp.einsum("bcign,bcgrpn,bcigr->bcigrp", c_c, h_prev, jnp.exp(cs))
    y = (y_diag + y_off).reshape(bsz, n_chunks * q, SSM_HEADS, SSM_HEAD_DIM)[:, :seq]
    return y, h_final.reshape(bsz, SSM_HEADS, SSM_HEAD_DIM, SSM_STATE)


def mamba_mixer(z, xbc, dt_raw, conv0, ssm0, conv_w, conv_b, dt_bias, a_log, ssm_d, ssm_norm):
    bsz, seq = xbc.shape[0], xbc.shape[1]
    xpad = jnp.concatenate([conv0.astype(xbc.dtype), xbc], axis=1)
    conv = conv_b
    for i in range(CONV_WIDTH):
        conv = conv + xpad[:, i:i + seq] * conv_w[i]
    new_conv = xpad[:, seq:]
    act = jax.nn.silu(conv).astype(jnp.float32)
    nb = SSM_GROUPS * SSM_STATE
    xs = act[..., :SSM_INNER].reshape(bsz, seq, SSM_HEADS, SSM_HEAD_DIM)
    bm = act[..., SSM_INNER:SSM_INNER + nb].reshape(bsz, seq, SSM_GROUPS, SSM_STATE)
    cm = act[..., SSM_INNER + nb:].reshape(bsz, seq, SSM_GROUPS, SSM_STATE)
    dt = jax.nn.softplus(dt_raw.astype(jnp.float32) + dt_bias.astype(jnp.float32))
    a = -jnp.exp(a_log.astype(jnp.float32))
    y, h_final = ssd_chunked(xs, dt, a, bm, cm, ssm0.astype(jnp.float32))
    y = y + ssm_d.astype(jnp.float32)[:, None] * xs
    y = y.reshape(bsz, seq, SSM_INNER) * jax.nn.silu(z.astype(jnp.float32))
    yg = y.reshape(bsz, seq, SSM_GROUPS, SSM_INNER // SSM_GROUPS)
    yg = yg * lax.rsqrt(jnp.mean(yg * yg, axis=-1, keepdims=True) + EPS)
    y = yg.reshape(bsz, seq, SSM_INNER) * ssm_norm.astype(jnp.float32)
    return y.astype(z.dtype), new_conv, h_final


def layer_forward(x, pos, attend, h0_re, h0_im, conv0, ssm0, lw):
    bsz, seq = x.shape[0], x.shape[1]
    xn = rms_norm(x, lw["norm_mix"])
    h = xn @ lw["w_in"]
    u, q, k, v, iq, ik, iw, z, xbc, dt_raw, gate_logits = split_cols(h)
    q = rotary(rms_norm(q.reshape(bsz, seq, N_HEADS, HEAD_DIM), lw["q_norm"]), pos)
    k = rotary(rms_norm(k.reshape(bsz, seq, N_KV_HEADS, HEAD_DIM), lw["k_norm"]), pos)
    v = v.reshape(bsz, seq, N_KV_HEADS, HEAD_DIM)
    iq = rotary(iq.reshape(bsz, seq, IDX_HEADS, IDX_DIM), pos)
    ik = rotary(ik[:, :, None, :], pos)[:, :, 0, :]
    o_att = attend(q, k, v, iq, ik, iw)
    o_s5, s5_re, s5_im = s5_mixer(u, h0_re, h0_im, lw["s5_a_re"], lw["s5_a_im"], lw["s5_log_dt"],
                                  lw["s5_b_re"], lw["s5_b_im"], lw["s5_c_re"], lw["s5_c_im"],
                                  lw["s5_d"], lw["s5_w_glu"])
    o_ssm, conv_new, ssm_new = mamba_mixer(z, xbc, dt_raw, conv0, ssm0, lw["conv_w"], lw["conv_b"],
                                           lw["dt_bias"], lw["a_log"], lw["ssm_d"], lw["ssm_norm"])
    gates = jax.nn.sigmoid(gate_logits).reshape(bsz, seq, N_BRANCHES, D_MODEL)
    merged = (gates[:, :, 0] * (o_s5 @ lw["w_br_s5"])
              + gates[:, :, 1] * (o_att @ lw["w_br_att"])
              + gates[:, :, 2] * (o_ssm @ lw["w_br_ssm"]))
    x = x + merged @ lw["w_o"]
    xm = rms_norm(x, lw["norm_mlp"])
    x = x + jnp.square(jax.nn.relu(xm @ lw["w_up"])) @ lw["w_down"]
    return x, (k, v, ik, s5_re, s5_im, conv_new, ssm_new)


def setup_inputs(seed: int = 0) -> dict:
    key = jax.random.key(seed)
    keys = jax.random.split(key, 40)
    f32 = jnp.float32

    def nrm(i, shape, scale=1.0):
        return jax.random.normal(keys[i], shape, f32) * scale

    n_pages = PAST_LEN // PAGE_SIZE
    n_used = DEC_BATCH * n_pages
    n_pool = n_used + n_used // 4
    page_table = jax.random.permutation(keys[0], n_pool)[:n_used].reshape(DEC_BATCH, n_pages).astype(jnp.int32)
    dt_init = jnp.exp(jax.random.uniform(keys[25], (DEPTH, SSM_HEADS), f32, math.log(1e-3), math.log(1e-1)))
    return {
        "x_prompt": nrm(1, (BATCH, SEQ, D_MODEL)),
        "x_sample": nrm(2, (DEC_BATCH, DEC_SEQ, D_MODEL)),
        "cache_k": nrm(3, (DEPTH, n_pool, PAGE_SIZE, N_KV_HEADS, HEAD_DIM)),
        "cache_v": nrm(4, (DEPTH, n_pool, PAGE_SIZE, N_KV_HEADS, HEAD_DIM)),
        "cache_idx_k": nrm(5, (DEPTH, n_pool, PAGE_SIZE, IDX_DIM)),
        "state_s5_re": nrm(6, (DEPTH, DEC_BATCH, S5_GROUPS, S5_STATE), 0.3),
        "state_s5_im": nrm(7, (DEPTH, DEC_BATCH, S5_GROUPS, S5_STATE), 0.3),
        "state_conv": nrm(8, (DEPTH, DEC_BATCH, CONV_WIDTH - 1, CONV_DIM)),
        "state_ssm": nrm(9, (DEPTH, DEC_BATCH, SSM_HEADS, SSM_HEAD_DIM, SSM_STATE), 0.1),
        "page_table": page_table,
        "norm_mix": 1.0 + nrm(10, (DEPTH, D_MODEL), 0.02),
        "w_in": nrm(11, (DEPTH, D_MODEL, IN_DIM), D_MODEL ** -0.5),
        "q_norm": 1.0 + nrm(12, (DEPTH, HEAD_DIM), 0.02),
        "k_norm": 1.0 + nrm(13, (DEPTH, HEAD_DIM), 0.02),
        "s5_a_re": -0.5 + nrm(14, (DEPTH, S5_GROUPS, S5_STATE), 0.01),
        "s5_a_im": jnp.pi * jnp.arange(S5_STATE, dtype=f32) + nrm(15, (DEPTH, S5_GROUPS, S5_STATE), 0.01),
        "s5_log_dt": jax.random.uniform(keys[16], (DEPTH, S5_GROUPS), f32, math.log(1e-3), math.log(1e-1)),
        "s5_b_re": nrm(17, (DEPTH, S5_GROUPS, S5_STATE, S5_GROUP), (2 * S5_GROUP) ** -0.5),
        "s5_b_im": nrm(18, (DEPTH, S5_GROUPS, S5_STATE, S5_GROUP), (2 * S5_GROUP) ** -0.5),
        "s5_c_re": nrm(19, (DEPTH, S5_GROUPS, S5_GROUP, S5_STATE), S5_STATE ** -0.5),
        "s5_c_im": nrm(20, (DEPTH, S5_GROUPS, S5_GROUP, S5_STATE), S5_STATE ** -0.5),
        "s5_d": nrm(21, (DEPTH, S5_WIDTH)),
        "s5_w_glu": nrm(22, (DEPTH, S5_WIDTH, S5_WIDTH), S5_WIDTH ** -0.5),
        "conv_w": nrm(23, (DEPTH, CONV_WIDTH, CONV_DIM), CONV_WIDTH ** -0.5),
        "conv_b": nrm(24, (DEPTH, CONV_DIM), 0.02),
        "dt_bias": dt_init + jnp.log(-jnp.expm1(-dt_init)),
        "a_log": jnp.log(jax.random.uniform(keys[26], (DEPTH, SSM_HEADS), f32, 1.0, 16.0)),
        "ssm_d": 1.0 + nrm(27, (DEPTH, SSM_HEADS), 0.1),
        "ssm_norm": 1.0 + nrm(28, (DEPTH, SSM_INNER), 0.02),
        "w_br_s5": nrm(29, (DEPTH, S5_WIDTH, D_MODEL), S5_WIDTH ** -0.5),
        "w_br_att": nrm(30, (DEPTH, ATT_WIDTH, D_MODEL), ATT_WIDTH ** -0.5),
        "w_br_ssm": nrm(31, (DEPTH, SSM_INNER, D_MODEL), SSM_INNER ** -0.5),
        "w_o": nrm(32, (DEPTH, D_MODEL, D_MODEL), D_MODEL ** -0.5),
        "norm_mlp": 1.0 + nrm(33, (DEPTH, D_MODEL), 0.02),
        "w_up": nrm(34, (DEPTH, D_MODEL, D_FF), D_MODEL ** -0.5),
        "w_down": nrm(35, (DEPTH, D_FF, D_MODEL), D_FF ** -0.5),
    }


def reference(x_prompt, x_sample, cache_k, cache_v, cache_idx_k, state_s5_re, state_s5_im, state_conv,
              state_ssm, page_table, norm_mix, w_in, q_norm, k_norm, s5_a_re, s5_a_im, s5_log_dt,
              s5_b_re, s5_b_im, s5_c_re, s5_c_im, s5_d, s5_w_glu, conv_w, conv_b, dt_bias, a_log, ssm_d,
              ssm_norm, w_br_s5, w_br_att, w_br_ssm, w_o, norm_mlp, w_up, w_down):
    bsz_p, seq_p = x_prompt.shape[0], x_prompt.shape[1]
    past = page_table.shape[1] * PAGE_SIZE
    pos_p = jnp.arange(seq_p)
    pos_s = past + jnp.arange(x_sample.shape[1])
    xp, xs = x_prompt, x_sample
    st_p, st_s = [], []
    for l in range(DEPTH):
        lw = {
            "norm_mix": norm_mix[l], "w_in": w_in[l], "q_norm": q_norm[l], "k_norm": k_norm[l],
            "s5_a_re": s5_a_re[l], "s5_a_im": s5_a_im[l], "s5_log_dt": s5_log_dt[l],
            "s5_b_re": s5_b_re[l], "s5_b_im": s5_b_im[l], "s5_c_re": s5_c_re[l], "s5_c_im": s5_c_im[l],
            "s5_d": s5_d[l], "s5_w_glu": s5_w_glu[l], "conv_w": conv_w[l], "conv_b": conv_b[l],
            "dt_bias": dt_bias[l], "a_log": a_log[l], "ssm_d": ssm_d[l], "ssm_norm": ssm_norm[l],
            "w_br_s5": w_br_s5[l], "w_br_att": w_br_att[l], "w_br_ssm": w_br_ssm[l], "w_o": w_o[l],
            "norm_mlp": norm_mlp[l], "w_up": w_up[l], "w_down": w_down[l],
        }
        h0 = jnp.zeros((bsz_p, S5_GROUPS, S5_STATE), jnp.float32)
        conv0 = jnp.zeros((bsz_p, CONV_WIDTH - 1, CONV_DIM), x_prompt.dtype)
        ssm0 = jnp.zeros((bsz_p, SSM_HEADS, SSM_HEAD_DIM, SSM_STATE), jnp.float32)
        xp, sp = layer_forward(xp, pos_p, dsa_prompt, h0, h0, conv0, ssm0, lw)
        attend_s = functools.partial(dsa_sample, cache_k=cache_k[l], cache_v=cache_v[l],
                                     cache_ik=cache_idx_k[l], page_table=page_table)
        xs, ss = layer_forward(xs, pos_s, attend_s, state_s5_re[l], state_s5_im[l], state_conv[l],
                               state_ssm[l], lw)
        st_p.append(sp)
        st_s.append(ss)
    new_k_prompt = jnp.stack([s[0] for s in st_p])
    new_v_prompt = jnp.stack([s[1] for s in st_p])
    new_idx_k_prompt = jnp.stack([s[2] for s in st_p])
    new_s5_re_prompt = jnp.stack([s[3] for s in st_p])
    new_s5_im_prompt = jnp.stack([s[4] for s in st_p])
    new_conv_prompt = jnp.stack([s[5] for s in st_p])
    new_ssm_prompt = jnp.stack([s[6] for s in st_p])
    new_k_sample = jnp.stack([s[0] for s in st_s])
    new_v_sample = jnp.stack([s[1] for s in st_s])
    new_idx_k_sample = jnp.stack([s[2] for s in st_s])
    new_s5_re_sample = jnp.stack([s[3] for s in st_s])
    new_s5_im_sample = jnp.stack([s[4] for s in st_s])
    new_conv_sample = jnp.stack([s[5] for s in st_s])
    new_ssm_sample = jnp.stack([s[6] for s in st_s])
    return (xp, xs, new_k_prompt, new_v_prompt, new_idx_k_prompt, new_s5_re_prompt, new_s5_im_prompt,
            new_conv_prompt, new_ssm_prompt, new_k_sample, new_v_sample, new_idx_k_sample, new_s5_re_sample,
            new_s5_im_sample, new_conv_sample, new_ssm_sample)
```

```python
import functools
import math

import jax
import jax.numpy as jnp
from jax import lax
from jax.experimental import pallas as pl
from jax.experimental.pallas import tpu as pltpu

F32 = jnp.float32
BF16 = jnp.bfloat16
I32 = jnp.int32

D_MODEL = 1024
EPS = 1e-6
S5_GROUP = 16
S5_WIDTH = 512
S5_GROUPS = 32
S5_STATE = 64
S5_LANES = S5_GROUPS * S5_STATE
N_HEADS = 8
N_KV_HEADS = 4
HEAD_DIM = 64
ATT_WIDTH = 512
KV_WIDTH = 256
ROT_DIM = 16
ROPE_THETA = 500000.0
IDX_HEADS = 4
IDX_DIM = 64
IDX_SCALE = (IDX_HEADS * IDX_DIM) ** -0.5
TOPK_MAX = 256
Q_BLOCK = 128
PAGE_SIZE = 128
SSM_INNER = 1024
SSM_HEAD_DIM = 64
SSM_HEADS = 16
SSM_GROUPS = 2
SSM_STATE = 128
CONV_WIDTH = 4
CONV_DIM = 1536
SSD_CHUNK = 128
D_FF = 4096
IN_SIZES = (512, 512, 256, 256, 256, 64, 4, 1024, 1536, 16, 3072)

LANES = 128
SUBLANES = 8
VMEM_LIMIT = 56 * 1024 * 1024

NP = 7680
OFF_GATES, OFF_Z, OFF_U, OFF_XBC, OFF_Q, OFF_K, OFF_V, OFF_IQ, OFF_IKW, OFF_DT = (
    0, 3072, 4096, 4608, 6144, 6656, 6912, 7168, 7424, 7552)

NEG_BIG = -1e30
INT_MIN = -(2 ** 31)


def _cparams(*sem):
    return pltpu.CompilerParams(dimension_semantics=sem, vmem_limit_bytes=VMEM_LIMIT)


def _split3(x):
    hi = x.astype(BF16)
    r = x - hi.astype(F32)
    mid = r.astype(BF16)
    lo = (r - mid.astype(F32)).astype(BF16)
    return hi, mid, lo


def _dot(a, b):
    return jnp.dot(a, b, preferred_element_type=F32)


def _dot_nt(a, b):
    return lax.dot_general(a, b, (((1,), (1,)), ((), ())), preferred_element_type=F32)


def _dot_tn(a, b):
    return lax.dot_general(a, b, (((0,), (0,)), ((), ())), preferred_element_type=F32)


def _dot3_exact_rhs(x, m_bf16):
    hi, mid, lo = _split3(x)
    return _dot(hi, m_bf16) + _dot(mid, m_bf16) + _dot(lo, m_bf16)


def _dot3_exact_lhs(m_bf16, x):
    hi, mid, lo = _split3(x)
    return _dot(m_bf16, hi) + _dot(m_bf16, mid) + _dot(m_bf16, lo)


def _inproj_kernel(x_ref, g_ref, w_ref, o_ref, xn_ref):
    @pl.when(pl.program_id(1) == 0)
    def _():
        x = x_ref[...]
        ms = jnp.mean(x * x, axis=-1, keepdims=True)
        xn_ref[...] = (x * lax.rsqrt(ms + EPS) * g_ref[...]).astype(BF16)

    o_ref[...] = _dot(xn_ref[...], w_ref[...])


def _inproj(x, g, wp, tm, tn=1536):
    m = x.shape[0]
    return pl.pallas_call(
        _inproj_kernel,
        grid=(m // tm, NP // tn),
        in_specs=[
            pl.BlockSpec((tm, D_MODEL), lambda i, j: (i, 0)),
            pl.BlockSpec((1, D_MODEL), lambda i, j: (0, 0)),
            pl.BlockSpec((D_MODEL, tn), lambda i, j: (0, j)),
        ],
        out_specs=pl.BlockSpec((tm, tn), lambda i, j: (i, j)),
        out_shape=jax.ShapeDtypeStruct((m, NP), F32),
        scratch_shapes=[pltpu.VMEM((tm, D_MODEL), BF16)],
        compiler_params=_cparams("parallel", "arbitrary"),
        name="inproj",
    )(x, g, wp)


def _rope(x, c, sa, sb):
    outs = []
    for t in range(x.shape[1] // LANES):
        xt = x[:, t * LANES:(t + 1) * LANES]
        outs.append(xt * c + pltpu.roll(xt, LANES - ROT_DIM // 2, 1) * sa + pltpu.roll(xt, ROT_DIM // 2, 1) * sb)
    return outs[0] if len(outs) == 1 else jnp.concatenate(outs, axis=1)


def _head_rms(x, bd_ref, gain):
    ms = _dot3_exact_rhs(x * x, bd_ref[...]) * (1.0 / HEAD_DIM)
    return x * lax.rsqrt(ms + EPS) * gain


def _post_kernel(q_ref, k_ref, v_ref, iq_ref, ikw_ref, tab_ref, qg_ref, kg_ref, bdq_ref, bdk_ref,
                 qb_ref, ko_ref, kb_ref, vo_ref, vb_ref, iqh_ref, iql_ref, iko_ref, ikh_ref, ikl_ref):
    c = tab_ref[:, 0:LANES]
    sa = tab_ref[:, LANES:2 * LANES]
    sb = tab_ref[:, 2 * LANES:3 * LANES]
    q = _rope(_head_rms(q_ref[...], bdq_ref, qg_ref[...]), c, sa, sb)
    qb_ref[...] = (q * (HEAD_DIM ** -0.5)).astype(BF16)
    k = _rope(_head_rms(k_ref[...], bdk_ref, kg_ref[...]), c, sa, sb)
    ko_ref[...] = k
    kb_ref[...] = k.astype(BF16)
    v = v_ref[...]
    vo_ref[...] = v
    vb_ref[...] = v.astype(BF16)
    iq = _rope(iq_ref[...], c, sa, sb)
    iqh = iq.astype(BF16)
    iqh_ref[...] = iqh
    iql_ref[...] = (iq - iqh.astype(F32)).astype(BF16)
    ik = _rope(ikw_ref[...], c, sa, sb)[:, 0:IDX_DIM]
    iko_ref[...] = ik
    ikh = ik.astype(BF16)
    ikh_ref[...] = ikh
    ikl_ref[...] = (ik - ikh.astype(F32)).astype(BF16)


def _post(h, tab, qg, kg, bdq, bdk, nb, lt, tt):
    m = nb * lt
    nt = lt // tt

    def col(width, off):
        return pl.BlockSpec((tt, width), lambda b, i: (b * nt + i, off // width))

    def out(width):
        return pl.BlockSpec((tt, width), lambda b, i: (b * nt + i, 0))

    def full(shape):
        return pl.BlockSpec(shape, lambda b, i: (0, 0))

    shapes = [
        (ATT_WIDTH, BF16), (KV_WIDTH, F32), (KV_WIDTH, BF16), (KV_WIDTH, F32), (KV_WIDTH, BF16),
        (KV_WIDTH, BF16), (KV_WIDTH, BF16), (IDX_DIM, F32), (IDX_DIM, BF16), (IDX_DIM, BF16)]
    return pl.pallas_call(
        _post_kernel,
        grid=(nb, nt),
        in_specs=[
            col(ATT_WIDTH, OFF_Q), col(KV_WIDTH, OFF_K), col(KV_WIDTH, OFF_V), col(KV_WIDTH, OFF_IQ),
            col(LANES, OFF_IKW),
            pl.BlockSpec((tt, 3 * LANES), lambda b, i: (i, 0)),
            full((1, ATT_WIDTH)), full((1, KV_WIDTH)), full((ATT_WIDTH, ATT_WIDTH)), full((KV_WIDTH, KV_WIDTH)),
        ],
        out_specs=[out(w) for w, _ in shapes],
        out_shape=[jax.ShapeDtypeStruct((m, w), d) for w, d in shapes],
        compiler_params=_cparams("parallel", "parallel"),
        name="qk_post",
    )(h, h, h, h, h, tab, qg, kg, bdq, bdk)


def _sort_key(score):
    score = jnp.where(score == 0.0, 0.0, score)
    bits = pltpu.bitcast(score, I32)
    return bits ^ (lax.shift_right_arithmetic(bits, 31) & 0x7FFFFFFF)


def _lane_fold(x):
    acc = x[:, 0:LANES]
    for t in range(1, x.shape[1] // LANES):
        acc = acc + x[:, t * LANES:(t + 1) * LANES]
    return acc


def _topk_threshold(key_ref, nkc, kc, rows, k_sel, idx_bits):
    def count(pred):
        def body(c, acc):
            return acc + _lane_fold(pred(key_ref[c], c).astype(I32))
        acc = lax.fori_loop(0, nkc, body, jnp.zeros((rows, LANES), I32))
        return jnp.sum(acc, axis=1, keepdims=True)

    def bit_body(i, carry):
        t_u, cnt_t = carry
        cand_u = t_u | lax.shift_left(jnp.int32(1), 31 - i)
        cand_s = cand_u ^ INT_MIN
        cnt = count(lambda kk, c: kk >= cand_s)
        take = cnt >= k_sel
        return jnp.where(take, cand_u, t_u), jnp.where(take, cnt, cnt_t)

    t_u0 = jnp.zeros((rows, 1), I32)
    cnt0 = jnp.full((rows, 1), k_sel, I32)
    t_u, cnt_t = lax.fori_loop(0, 32, bit_body, (t_u0, cnt0))
    t = t_u ^ INT_MIN
    has_ties = jnp.max(jnp.where(cnt_t > k_sel, 1, 0)) > 0

    def tie_search(_):
        need_m1 = k_sel - 1 - count(lambda kk, c: kk > t)

        def pos_of(c):
            return c * kc + lax.broadcasted_iota(I32, (rows, kc), 1)

        def jbody(i, j):
            cand = j | lax.shift_left(jnp.int32(1), idx_bits - 1 - i)
            f = count(lambda kk, c: (kk == t) & (pos_of(c) < cand))
            return jnp.where(f <= need_m1, cand, j)

        return lax.fori_loop(0, idx_bits, jbody, jnp.zeros((rows, 1), I32))

    j = lax.cond(has_ties, tie_search, lambda _: jnp.full((rows, 1), 2 ** 30, I32), 0)
    return t, j


def _pad_queries(q):
    lane = lax.broadcasted_iota(I32, (Q_BLOCK, LANES), 1)
    zero = jnp.zeros((Q_BLOCK, LANES), q.dtype)
    out = []
    for g in range(N_KV_HEADS):
        tile = q[:, g * LANES:(g + 1) * LANES]
        rolled = pltpu.roll(tile.astype(F32), HEAD_DIM, 1).astype(q.dtype)
        rows = []
        for r in range(2):
            src = tile if (g % 2) == r else rolled
            lo = (g % 2) * HEAD_DIM
            keep = (lane >= lo) & (lane < lo + HEAD_DIM)
            placed = jnp.where(keep, src, zero)
            rows.append(jnp.concatenate([placed, zero] if g < 2 else [zero, placed], axis=1))
        out.append(jnp.concatenate(rows, axis=0))
    return out


def _dsa_kernel(qb_ref, iqh_ref, iql_ref, ikw_ref, kb_ref, vb_ref, ikh_ref, ikl_ref, o_ref,
                key_ref, m_ref, l_ref, acc_ref, *, kc, k_sel, idx_bits):
    qi = pl.program_id(1)
    nkc = (qi * Q_BLOCK) // kc + 1
    qpos = qi * Q_BLOCK + lax.broadcasted_iota(I32, (Q_BLOCK, 1), 0)

    def kpos(c):
        return c * kc + lax.broadcasted_iota(I32, (Q_BLOCK, kc), 1)

    iqh = iqh_ref[...]
    iql = iql_ref[...]
    lh = jnp.concatenate([iqh[:, h * IDX_DIM:(h + 1) * IDX_DIM] for h in range(IDX_HEADS)], axis=0)
    ll = jnp.concatenate([iql[:, h * IDX_DIM:(h + 1) * IDX_DIM] for h in range(IDX_HEADS)], axis=0)
    iw = ikw_ref[...]
    iws = [iw[:, IDX_DIM + h:IDX_DIM + h + 1] for h in range(IDX_HEADS)]

    def idx_body(c, _):
        r0 = pl.multiple_of(c * kc, kc)
        kh = ikh_ref[pl.ds(r0, kc), :]
        kl = ikl_ref[pl.ds(r0, kc), :]
        d = jnp.maximum(_dot_nt(lh, kh) + _dot_nt(lh, kl) + _dot_nt(ll, kh), 0.0)
        sc = iws[0] * d[0:Q_BLOCK]
        for h in range(1, IDX_HEADS):
            sc = sc + iws[h] * d[h * Q_BLOCK:(h + 1) * Q_BLOCK]
        sc = jnp.where(kpos(c) <= qpos, sc * IDX_SCALE, -jnp.inf)
        key_ref[c] = _sort_key(sc)
        return 0

    lax.fori_loop(0, nkc, idx_body, 0)
    t, jcut = _topk_threshold(key_ref, nkc, kc, Q_BLOCK, k_sel, idx_bits)

    qpad = _pad_queries(qb_ref[...])
    m_ref[...] = jnp.full(m_ref.shape, NEG_BIG, F32)
    l_ref[...] = jnp.zeros(l_ref.shape, F32)
    acc_ref[...] = jnp.zeros(acc_ref.shape, F32)

    def att_body(c, _):
        r0 = pl.multiple_of(c * kc, kc)
        kblk = kb_ref[pl.ds(r0, kc), :]
        vblk = vb_ref[pl.ds(r0, kc), :]
        kk = key_ref[c]
        pos = kpos(c)
        sel = ((kk > t) | ((kk == t) & (pos <= jcut))) & (pos <= qpos)
        sel2 = jnp.concatenate([sel, sel], axis=0)
        for g in range(N_KV_HEADS):
            s = jnp.where(sel2, _dot_nt(qpad[g], kblk), NEG_BIG)
            m_old = m_ref[g]
            m_new = jnp.maximum(m_old, jnp.max(s, axis=1, keepdims=True))
            alpha = jnp.exp(m_old - m_new)
            p = jnp.exp(s - m_new)
            l_ref[g] = alpha * l_ref[g] + jnp.sum(p, axis=1, keepdims=True)
            acc_ref[g] = alpha * acc_ref[g] + _dot(p.astype(BF16), vblk)
            m_ref[g] = m_new
        return 0

    lax.fori_loop(0, nkc, att_body, 0)

    lane = lax.broadcasted_iota(I32, (2 * Q_BLOCK, KV_WIDTH), 1)
    res = jnp.zeros((2 * Q_BLOCK, KV_WIDTH), F32)
    for g in range(N_KV_HEADS):
        keep = (lane >= g * HEAD_DIM) & (lane < (g + 1) * HEAD_DIM)
        res = res + jnp.where(keep, acc_ref[g] / l_ref[g], 0.0)
    o_ref[...] = jnp.concatenate([res[0:Q_BLOCK], res[Q_BLOCK:]], axis=1)


def _dsa_prompt(qb, iqh, iql, h, kb, vb, ikh, ikl, nb, lt):
    nq = lt // Q_BLOCK
    kc = min(512, lt)
    k_sel = min(TOPK_MAX, lt // 4)
    idx_bits = max(1, (lt - 1).bit_length())

    def qblk(width, off=0):
        return pl.BlockSpec((Q_BLOCK, width), lambda b, i: (b * nq + i, off // width))

    def seq(width):
        return pl.BlockSpec((lt, width), lambda b, i: (b, 0))

    return pl.pallas_call(
        functools.partial(_dsa_kernel, kc=kc, k_sel=k_sel, idx_bits=idx_bits),
        grid=(nb, nq),
        in_specs=[qblk(ATT_WIDTH), qblk(KV_WIDTH), qblk(KV_WIDTH), qblk(LANES, OFF_IKW),
                  seq(KV_WIDTH), seq(KV_WIDTH), seq(IDX_DIM), seq(IDX_DIM)],
        out_specs=qblk(ATT_WIDTH),
        out_shape=jax.ShapeDtypeStruct((nb * lt, ATT_WIDTH), F32),
        scratch_shapes=[
            pltpu.VMEM((lt // kc, Q_BLOCK, kc), I32),
            pltpu.VMEM((N_KV_HEADS, 2 * Q_BLOCK, 1), F32),
            pltpu.VMEM((N_KV_HEADS, 2 * Q_BLOCK, 1), F32),
            pltpu.VMEM((N_KV_HEADS, 2 * Q_BLOCK, KV_WIDTH), F32),
        ],
        compiler_params=_cparams("parallel", "arbitrary"),
        name="dsa_prompt",
    )(qb, iqh, iql, h, kb.reshape(nb * lt, KV_WIDTH), vb.reshape(nb * lt, KV_WIDTH),
      ikh.reshape(nb * lt, IDX_DIM), ikl.reshape(nb * lt, IDX_DIM))


S5_LANE_BLOCK = 512


S5_TILES = S5_WIDTH // LANES


def _s5_permute_in(u_ref, up_ref, tt):
    if tt == 1:
        return u_ref[:, 0, :]
    for j in range(SUBLANES):
        uj = u_ref[j]
        for k in range(S5_TILES):
            up_ref[k, pl.ds(j, tt, stride=SUBLANES), :] = uj[:, k * LANES:(k + 1) * LANES]
    return jnp.concatenate([up_ref[k] for k in range(S5_TILES)], axis=1)


def _s5_permute_out(res, y_ref, o_ref, tt):
    if tt == 1:
        o_ref[:, 0, :] = res
        return
    for k in range(S5_TILES):
        y_ref[k] = res[:, k * LANES:(k + 1) * LANES]
    for j in range(SUBLANES):
        o_ref[j] = jnp.concatenate(
            [y_ref[k, pl.ds(j, tt, stride=SUBLANES), :] for k in range(S5_TILES)], axis=1)


def _s5_scan(bu_ref, ar_ref, ai_ref, hr_ref, hi_ref, tt, store):
    for lb in range(S5_LANES // S5_LANE_BLOCK):
        re = slice(lb * S5_LANE_BLOCK, (lb + 1) * S5_LANE_BLOCK)
        im = slice(S5_LANES + lb * S5_LANE_BLOCK, S5_LANES + (lb + 1) * S5_LANE_BLOCK)
        ar = jnp.broadcast_to(ar_ref[:, re], (SUBLANES, S5_LANE_BLOCK))
        ai = jnp.broadcast_to(ai_ref[:, re], (SUBLANES, S5_LANE_BLOCK))

        def step(t, carry):
            hr, hi = carry
            r0 = pl.multiple_of(t * SUBLANES, SUBLANES)
            nhr = ar * hr - ai * hi + bu_ref[pl.ds(r0, SUBLANES), re]
            nhi = ar * hi + ai * hr + bu_ref[pl.ds(r0, SUBLANES), im]
            if store:
                bu_ref[pl.ds(r0, SUBLANES), re] = nhr
                bu_ref[pl.ds(r0, SUBLANES), im] = nhi
            return nhr, nhi

        hr, hi = lax.fori_loop(0, tt, step, (hr_ref[:, re], hi_ref[:, re]), unroll=min(tt, 8))
        hr_ref[:, re] = hr
        hi_ref[:, re] = hi


def _s5_ends_kernel(u_ref, bmat_ref, ar_ref, ai_ref, e_ref, up_ref, bu_ref, hr_ref, hi_ref, *, tt):
    i = pl.program_id(1)

    @pl.when(i == 0)
    def _():
        hr_ref[...] = jnp.zeros(hr_ref.shape, F32)
        hi_ref[...] = jnp.zeros(hi_ref.shape, F32)

    up = _s5_permute_in(u_ref, up_ref, tt)
    bu_ref[...] = _dot(up.astype(BF16), bmat_ref[...])
    _s5_scan(bu_ref, ar_ref, ai_ref, hr_ref, hi_ref, tt, store=False)

    @pl.when(i == pl.num_programs(1) - 1)
    def _():
        e_ref[:, 0:S5_LANES] = hr_ref[...]
        e_ref[:, S5_LANES:] = hi_ref[...]


def _s5_carry_kernel(e_ref, pr_ref, pi_ref, h0_ref):
    e = e_ref[...]
    pr = pr_ref[...]
    pi = pi_ref[...]
    cr = jnp.zeros((1, S5_LANES), F32)
    ci = jnp.zeros((1, S5_LANES), F32)
    for j in range(SUBLANES):
        h0_ref[j:j + 1, 0:S5_LANES] = cr
        h0_ref[j:j + 1, S5_LANES:] = ci
        er = e[j:j + 1, 0:S5_LANES]
        ei = e[j:j + 1, S5_LANES:]
        cr, ci = pr * cr - pi * ci + er, pr * ci + pi * cr + ei


def _s5_main_kernel(u_ref, h0_ref, bmat_ref, cmat_ref, ar_ref, ai_ref, d_ref, wglu_ref, o_ref, hl_ref,
                    up_ref, bu_ref, y_ref, hr_ref, hi_ref, *, tt):
    i = pl.program_id(1)

    @pl.when(i == 0)
    def _():
        hr_ref[...] = h0_ref[:, 0:S5_LANES]
        hi_ref[...] = h0_ref[:, S5_LANES:]

    up = _s5_permute_in(u_ref, up_ref, tt)
    bu_ref[...] = _dot(up.astype(BF16), bmat_ref[...])
    _s5_scan(bu_ref, ar_ref, ai_ref, hr_ref, hi_ref, tt, store=True)

    @pl.when(i == pl.num_programs(1) - 1)
    def _():
        hl_ref[:, 0:S5_LANES] = hr_ref[...]
        hl_ref[:, S5_LANES:] = hi_ref[...]

    y = _dot(bu_ref[...].astype(BF16), cmat_ref[...]) + d_ref[...] * up
    g = jax.nn.gelu(y)
    res = g * jax.nn.sigmoid(_dot(g.astype(BF16), wglu_ref[...]))
    _s5_permute_out(res, y_ref, o_ref, tt)


def _s5_specs(nb, lseg, tt, col):
    u_spec = pl.BlockSpec((None, SUBLANES, tt, S5_WIDTH), lambda b, i: (b, 0, i, col))
    st_spec = pl.BlockSpec((None, SUBLANES, 2 * S5_LANES), lambda b, i: (b, 0, 0))

    def full(shape):
        return pl.BlockSpec(shape, lambda b, i: (0,) * len(shape))

    return u_spec, st_spec, full


def _s5_ends(u4, col, bmat, ar, ai, tt):
    nb, _, lseg, _ = u4.shape
    u_spec, st_spec, full = _s5_specs(nb, lseg, tt, col)
    return pl.pallas_call(
        functools.partial(_s5_ends_kernel, tt=tt),
        grid=(nb, lseg // tt),
        in_specs=[u_spec, full((S5_WIDTH, 2 * S5_LANES)), full((1, S5_LANES)), full((1, S5_LANES))],
        out_specs=st_spec,
        out_shape=jax.ShapeDtypeStruct((nb, SUBLANES, 2 * S5_LANES), F32),
        scratch_shapes=[
            pltpu.VMEM((S5_TILES, tt * SUBLANES, LANES), F32),
            pltpu.VMEM((tt * SUBLANES, 2 * S5_LANES), F32),
            pltpu.VMEM((SUBLANES, S5_LANES), F32),
            pltpu.VMEM((SUBLANES, S5_LANES), F32),
        ],
        compiler_params=_cparams("parallel", "arbitrary"),
        name="s5_ends",
    )(u4, bmat, ar, ai)


def _s5_carry(e, pr, pi):
    nb = e.shape[0]
    st_spec = pl.BlockSpec((None, SUBLANES, 2 * S5_LANES), lambda b: (b, 0, 0))
    p_spec = pl.BlockSpec((1, S5_LANES), lambda b: (0, 0))
    return pl.pallas_call(
        _s5_carry_kernel,
        grid=(nb,),
        in_specs=[st_spec, p_spec, p_spec],
        out_specs=st_spec,
        out_shape=jax.ShapeDtypeStruct(e.shape, F32),
        compiler_params=_cparams("parallel"),
        name="s5_carry",
    )(e, pr, pi)


def _s5_main(u4, col, h0, bmat, cmat, ar, ai, dskip, wglu, tt):
    nb, _, lseg, _ = u4.shape
    u_spec, st_spec, full = _s5_specs(nb, lseg, tt, col)
    o_spec = pl.BlockSpec((None, SUBLANES, tt, S5_WIDTH), lambda b, i: (b, 0, i, 0))
    return pl.pallas_call(
        functools.partial(_s5_main_kernel, tt=tt),
        grid=(nb, lseg // tt),
        in_specs=[u_spec, st_spec, full((S5_WIDTH, 2 * S5_LANES)), full((2 * S5_LANES, S5_WIDTH)),
                  full((1, S5_LANES)), full((1, S5_LANES)), full((1, S5_WIDTH)), full((S5_WIDTH, S5_WIDTH))],
        out_specs=[o_spec, st_spec],
        out_shape=[jax.ShapeDtypeStruct((nb, SUBLANES, lseg, S5_WIDTH), F32),
                   jax.ShapeDtypeStruct((nb, SUBLANES, 2 * S5_LANES), F32)],
        scratch_shapes=[
            pltpu.VMEM((S5_TILES, tt * SUBLANES, LANES), F32),
            pltpu.VMEM((tt * SUBLANES, 2 * S5_LANES), F32),
            pltpu.VMEM((S5_TILES, tt * SUBLANES, LANES), F32),
            pltpu.VMEM((SUBLANES, S5_LANES), F32),
            pltpu.VMEM((SUBLANES, S5_LANES), F32),
        ],
        compiler_params=_cparams("parallel", "arbitrary"),
        name="s5_main",
    )(u4, h0, bmat, cmat, ar, ai, dskip, wglu)


def _softplus(x):
    return jnp.maximum(x, 0.0) + jnp.log1p(jnp.exp(-jnp.abs(x)))


def _silu(x):
    return x * jax.nn.sigmoid(x)


def _ssd_gate_norm(y, z, nrm):
    y = y * _silu(z)
    outs = []
    gw = SSM_INNER // SSM_GROUPS
    for g in range(SSM_GROUPS):
        yg = y[:, g * gw:(g + 1) * gw]
        ms = jnp.mean(yg * yg, axis=-1, keepdims=True)
        outs.append(yg * lax.rsqrt(ms + EPS))
    return jnp.concatenate(outs, axis=1) * nrm


def _ssd_kernel(xbc_ref, z_ref, dt_ref, cw_ref, cb_ref, dtb_ref, alog_ref, dexp_ref, nrm_ref, exp_ref,
                o_ref, st_ref, xpad_ref, stt_ref):
    c = pl.program_id(1)
    t = SSD_CHUNK
    gw = SSM_INNER // SSM_GROUPS

    @pl.when(c == 0)
    def _():
        xpad_ref[0:SUBLANES, :] = jnp.zeros((SUBLANES, CONV_DIM), F32)
        stt_ref[...] = jnp.zeros(stt_ref.shape, F32)

    x = xbc_ref[...]
    xpad_ref[SUBLANES:SUBLANES + t, :] = x
    conv = cb_ref[...]
    for k in range(CONV_WIDTH):
        conv = conv + xpad_ref[pl.ds(SUBLANES - (CONV_WIDTH - 1) + k, t), :] * cw_ref[k:k + 1, :]
    xpad_ref[0:SUBLANES, :] = x[t - SUBLANES:t, :]
    act = _silu(conv)
    xs = act[:, 0:SSM_INNER]
    bm = act[:, SSM_INNER:SSM_INNER + SSM_GROUPS * SSM_STATE]
    cm = act[:, SSM_INNER + SSM_GROUPS * SSM_STATE:]

    dt = _softplus(dt_ref[...] + dtb_ref[...])
    a = -jnp.exp(alog_ref[...])
    row = lax.broadcasted_iota(I32, (t, t), 0)
    colm = lax.broadcasted_iota(I32, (t, t), 1)
    causal = row >= colm
    ltri = jnp.where(causal, 1.0, 0.0).astype(BF16)
    cs = _dot3_exact_lhs(ltri, dt * a)
    cs_t = cs.T
    dt_t = dt.T
    ecs = jnp.exp(cs)
    wgt = jnp.exp(cs[t - 1:t, :] - cs) * dt
    expand = exp_ref[...]
    ecs_x = _dot3_exact_rhs(ecs, expand)
    wgt_x = _dot3_exact_rhs(wgt, expand)

    lane = lax.broadcasted_iota(I32, (t, LANES), 1)
    lo_half = lane < SSM_HEAD_DIM
    xs_b = xs.astype(BF16)
    zero_b = jnp.zeros((t, LANES), BF16)
    y_tiles = []
    for hp in range(SSM_HEADS // 2):
        g = hp // (SSM_HEADS // 2 // SSM_GROUPS)
        cb = _dot_nt(cm[:, g * SSM_STATE:(g + 1) * SSM_STATE].astype(BF16),
                     bm[:, g * SSM_STATE:(g + 1) * SSM_STATE].astype(BF16))
        xt = xs_b[:, hp * LANES:(hp + 1) * LANES]
        acc = None
        for hh in range(2):
            h = 2 * hp + hh
            seg = cs[:, h:h + 1] - cs_t[h:h + 1, :]
            dec = jnp.exp(jnp.where(causal, seg, -jnp.inf))
            w = (cb * dec * dt_t[h:h + 1, :]).astype(BF16)
            xm = jnp.where(lo_half if hh == 0 else ~lo_half, xt, zero_b)
            part = _dot(w, xm)
            acc = part if acc is None else acc + part
        y_tiles.append(acc)
    y = jnp.concatenate(y_tiles, axis=1)

    xw = (xs * wgt_x).astype(BF16)
    cdec = ecs_x[t - 1:t, :]
    y_off = []
    for g in range(SSM_GROUPS):
        st = stt_ref[g]
        cg = cm[:, g * SSM_STATE:(g + 1) * SSM_STATE].astype(BF16)
        y_off.append(_dot(cg, st.astype(BF16)) * ecs_x[:, g * gw:(g + 1) * gw])
        bg = bm[:, g * SSM_STATE:(g + 1) * SSM_STATE].astype(BF16)
        stt_ref[g] = st * cdec[:, g * gw:(g + 1) * gw] + _dot_tn(bg, xw[:, g * gw:(g + 1) * gw])
    y = y + jnp.concatenate(y_off, axis=1) + dexp_ref[...] * xs
    o_ref[...] = _ssd_gate_norm(y, z_ref[...], nrm_ref[...])

    @pl.when(c == pl.num_programs(1) - 1)
    def _():
        for g in range(SSM_GROUPS):
            st_ref[g * gw:(g + 1) * gw, :] = stt_ref[g].T


def _ssd_prompt(h, cw, cb, dtb, alog, dexp, nrm, expand, nb, lt):
    nc = lt // SSD_CHUNK

    def col(width, off):
        return pl.BlockSpec((SSD_CHUNK, width), lambda b, c: (b * nc + c, off // width))

    def full(shape):
        return pl.BlockSpec(shape, lambda b, c: (0, 0))

    return pl.pallas_call(
        _ssd_kernel,
        grid=(nb, nc),
        in_specs=[col(CONV_DIM, OFF_XBC), col(SSM_INNER, OFF_Z), col(LANES, OFF_DT),
                  full((CONV_WIDTH, CONV_DIM)), full((1, CONV_DIM)), full((1, LANES)), full((1, LANES)),
                  full((1, SSM_INNER)), full((1, SSM_INNER)), full((LANES, SSM_INNER))],
        out_specs=[pl.BlockSpec((SSD_CHUNK, SSM_INNER), lambda b, c: (b * nc + c, 0)),
                   pl.BlockSpec((None, SSM_INNER, SSM_STATE), lambda b, c: (b, 0, 0))],
        out_shape=[jax.ShapeDtypeStruct((nb * lt, SSM_INNER), F32),
                   jax.ShapeDtypeStruct((nb, SSM_INNER, SSM_STATE), F32)],
        scratch_shapes=[pltpu.VMEM((SUBLANES + SSD_CHUNK, CONV_DIM), F32),
                        pltpu.VMEM((SSM_GROUPS, SSM_STATE, SSM_INNER // SSM_GROUPS), F32)],
        compiler_params=_cparams("parallel", "arbitrary"),
        name="ssd_prompt",
    )(h, h, h, cw, cb, dtb, alog, dexp, nrm, expand)


def _merge_kernel(x_ref, g0_ref, g1_ref, g2_ref, os5_ref, oatt_ref, ossm_ref, w1_ref, w2_ref, w3_ref, wo_ref,
                  o_ref):
    merged = (jax.nn.sigmoid(g0_ref[...]) * _dot(os5_ref[...].astype(BF16), w1_ref[...])
              + jax.nn.sigmoid(g1_ref[...]) * _dot(oatt_ref[...].astype(BF16), w2_ref[...])
              + jax.nn.sigmoid(g2_ref[...]) * _dot(ossm_ref[...].astype(BF16), w3_ref[...]))
    o_ref[...] = x_ref[...] + _dot(merged.astype(BF16), wo_ref[...])


def _merge(x, h, os5, oatt, ossm, w1, w2, w3, wo, tm):
    m = x.shape[0]

    def rows(width, blk=0):
        return pl.BlockSpec((tm, width), lambda i: (i, blk))

    def full(shape):
        return pl.BlockSpec(shape, lambda i: (0, 0))

    return pl.pallas_call(
        _merge_kernel,
        grid=(m // tm,),
        in_specs=[rows(D_MODEL), rows(D_MODEL, 0), rows(D_MODEL, 1), rows(D_MODEL, 2),
                  rows(S5_WIDTH), rows(ATT_WIDTH), rows(SSM_INNER),
                  full((S5_WIDTH, D_MODEL)), full((ATT_WIDTH, D_MODEL)), full((SSM_INNER, D_MODEL)),
                  full((D_MODEL, D_MODEL))],
        out_specs=rows(D_MODEL),
        out_shape=jax.ShapeDtypeStruct((m, D_MODEL), F32),
        compiler_params=_cparams("parallel"),
        name="merge",
    )(x, h, h, h, os5, oatt, ossm, w1, w2, w3, wo)


def _mlp_kernel(x_ref, g_ref, wu_ref, wd_ref, o_ref):
    x = x_ref[...]
    ms = jnp.mean(x * x, axis=-1, keepdims=True)
    xn = (x * lax.rsqrt(ms + EPS) * g_ref[...]).astype(BF16)
    hid = jnp.maximum(_dot(xn, wu_ref[...]), 0.0)
    o_ref[...] = x + _dot((hid * hid).astype(BF16), wd_ref[...])


def _mlp(x, g, wu, wd, tm):
    m = x.shape[0]
    return pl.pallas_call(
        _mlp_kernel,
        grid=(m // tm,),
        in_specs=[pl.BlockSpec((tm, D_MODEL), lambda i: (i, 0)),
                  pl.BlockSpec((1, D_MODEL), lambda i: (0, 0)),
                  pl.BlockSpec((D_MODEL, D_FF), lambda i: (0, 0)),
                  pl.BlockSpec((D_FF, D_MODEL), lambda i: (0, 0))],
        out_specs=pl.BlockSpec((tm, D_MODEL), lambda i: (i, 0)),
        out_shape=jax.ShapeDtypeStruct((m, D_MODEL), F32),
        compiler_params=_cparams("parallel"),
        name="mlp",
    )(x, g, wu, wd)


def _ssd_step_kernel(xbc_ref, z_ref, dt_ref, cst_ref, st_ref, cw_ref, cb_ref, dtb_ref, alog_ref, dexp_ref,
                     nrm_ref, exp_ref, o_ref, cst_out_ref, st_out_ref):
    ns = SUBLANES
    gw = SSM_INNER // SSM_GROUPS
    x = xbc_ref[...]
    conv = cb_ref[...]
    for k in range(CONV_WIDTH - 1):
        conv = conv + cst_ref[k] * cw_ref[k:k + 1, :]
    conv = conv + x * cw_ref[CONV_WIDTH - 1:CONV_WIDTH, :]
    for k in range(CONV_WIDTH - 2):
        cst_out_ref[k] = cst_ref[k + 1]
    cst_out_ref[CONV_WIDTH - 2] = x
    act = _silu(conv)
    xs = act[:, 0:SSM_INNER]
    bm = act[:, SSM_INNER:SSM_INNER + SSM_GROUPS * SSM_STATE]
    cm = act[:, SSM_INNER + SSM_GROUPS * SSM_STATE:]
    dt = _softplus(dt_ref[...] + dtb_ref[...])
    decay = jnp.exp(dt * (-jnp.exp(alog_ref[...])))
    expand = exp_ref[...]
    decay_x = _dot3_exact_rhs(decay, expand)
    xdt = xs * _dot3_exact_rhs(dt, expand)
    rowid = lax.broadcasted_iota(I32, (ns, 1), 0)
    ones = jnp.ones((ns, SSM_STATE), BF16)
    y = jnp.zeros((ns, SSM_INNER), F32)
    y_parts = [[], []]
    for s in range(ns):
        pick = rowid == s
        for g in range(SSM_GROUPS):
            cols = slice(g * gw, (g + 1) * gw)
            nsl = slice(g * SSM_STATE, (g + 1) * SSM_STATE)
            dsel = jnp.where(pick, decay_x[:, cols], 0.0)
            dh, dm, dl = _split3(dsel)
            dmat = _dot_tn(dh, ones) + _dot_tn(dm, ones) + _dot_tn(dl, ones)
            xsel = jnp.where(pick, xdt[:, cols], 0.0)
            xh = xsel.astype(BF16)
            xl = (xsel - xh.astype(F32)).astype(BF16)
            bg = bm[:, nsl].astype(BF16)
            new = dmat * st_ref[s, cols, :] + _dot_tn(xh, bg) + _dot_tn(xl, bg)
            st_out_ref[s, cols, :] = new
            csel = jnp.where(pick, cm[:, nsl], 0.0).astype(BF16)
            y_parts[g].append(_dot_nt(csel, new.astype(BF16)))
    halves = []
    for g in range(SSM_GROUPS):
        acc = y_parts[g][0]
        for part in y_parts[g][1:]:
            acc = acc + part
        halves.append(acc)
    y = jnp.concatenate(halves, axis=1) + dexp_ref[...] * xs
    o_ref[...] = _ssd_gate_norm(y, z_ref[...], nrm_ref[...])


def _ssd_step(h, cst, st, cw, cb, dtb, alog, dexp, nrm, expand):
    ns = h.shape[0]
    g8 = SUBLANES

    def col(width, off):
        return pl.BlockSpec((g8, width), lambda i: (i, off // width))

    def full(shape):
        return pl.BlockSpec(shape, lambda i: (0, 0))

    cst_spec = pl.BlockSpec((CONV_WIDTH - 1, g8, CONV_DIM), lambda i: (0, i, 0))
    st_spec = pl.BlockSpec((g8, SSM_INNER, SSM_STATE), lambda i: (i, 0, 0))
    return pl.pallas_call(
        _ssd_step_kernel,
        grid=(ns // g8,),
        in_specs=[col(CONV_DIM, OFF_XBC), col(SSM_INNER, OFF_Z), col(LANES, OFF_DT), cst_spec, st_spec,
                  full((CONV_WIDTH, CONV_DIM)), full((1, CONV_DIM)), full((1, LANES)), full((1, LANES)),
                  full((1, SSM_INNER)), full((1, SSM_INNER)), full((LANES, SSM_INNER))],
        out_specs=[pl.BlockSpec((g8, SSM_INNER), lambda i: (i, 0)), cst_spec, st_spec],
        out_shape=[jax.ShapeDtypeStruct((ns, SSM_INNER), F32),
                   jax.ShapeDtypeStruct(cst.shape, F32),
                   jax.ShapeDtypeStruct(st.shape, F32)],
        compiler_params=_cparams("parallel"),
        name="ssd_step",
    )(h, h, h, cst, st, cw, cb, dtb, alog, dexp, nrm, expand)


PAGES_PER_CHUNK = 16
SAMPLE_KC = PAGES_PER_CHUNK * PAGE_SIZE


def _dsa_sample_kernel(pt_ref, qb_ref, iqh_ref, iql_ref, ikw_ref, kn_ref, vn_ref, ikn_ref,
                       ck_hbm, cv_hbm, cik_hbm, o_ref,
                       ikbuf, kbuf, vbuf, key_ref, sem_ik, sem_kv, *, n_pages, page_base, k_sel, idx_bits):
    b = pl.program_id(0)
    kc = SAMPLE_KC
    n_chunks = n_pages // PAGES_PER_CHUNK
    past = n_pages * PAGE_SIZE

    def ik_copy(p):
        return pltpu.make_async_copy(cik_hbm.at[page_base + pt_ref[b, p]], ikbuf.at[p], sem_ik)

    def kv_copies(c, i):
        page = page_base + pt_ref[b, c * PAGES_PER_CHUNK + i]
        slot = c % 2
        return (pltpu.make_async_copy(ck_hbm.at[page], kbuf.at[slot, i], sem_kv.at[0, slot]),
                pltpu.make_async_copy(cv_hbm.at[page], vbuf.at[slot, i], sem_kv.at[1, slot]))

    def start_kv(c):
        def body(i, _):
            ck, cv = kv_copies(c, i)
            ck.start()
            cv.start()
            return 0
        lax.fori_loop(0, PAGES_PER_CHUNK, body, 0)

    def wait_kv(c):
        def body(i, _):
            ck, cv = kv_copies(c, i)
            ck.wait()
            cv.wait()
            return 0
        lax.fori_loop(0, PAGES_PER_CHUNK, body, 0)

    def start_ik(p, _):
        ik_copy(p).start()
        return 0

    def wait_ik(p, _):
        ik_copy(p).wait()
        return 0

    lax.fori_loop(0, n_pages, start_ik, 0)
    start_kv(0)
    lax.fori_loop(0, n_pages, wait_ik, 0)

    iqh = iqh_ref[...]
    iql = iql_ref[...]
    zrows = jnp.zeros((SUBLANES - IDX_HEADS, IDX_DIM), BF16)
    lh = jnp.concatenate([iqh[:, h * IDX_DIM:(h + 1) * IDX_DIM] for h in range(IDX_HEADS)] + [zrows], axis=0)
    ll = jnp.concatenate([iql[:, h * IDX_DIM:(h + 1) * IDX_DIM] for h in range(IDX_HEADS)] + [zrows], axis=0)
    iw = ikw_ref[...]
    iw_col = jnp.concatenate([iw[:, IDX_DIM + h:IDX_DIM + h + 1] for h in range(IDX_HEADS)]
                             + [jnp.zeros((SUBLANES - IDX_HEADS, 1), F32)], axis=0)

    def score_of(kh, kl):
        d = jnp.maximum(_dot_nt(lh, kh) + _dot_nt(lh, kl) + _dot_nt(ll, kh), 0.0)
        return jnp.sum(iw_col * d, axis=0, keepdims=True) * IDX_SCALE

    def idx_body(c, _):
        p0 = pl.multiple_of(c * PAGES_PER_CHUNK, PAGES_PER_CHUNK)
        ik = ikbuf[pl.ds(p0, PAGES_PER_CHUNK)].reshape(kc, IDX_DIM)
        kh = ik.astype(BF16)
        kl = (ik - kh.astype(F32)).astype(BF16)
        key_ref[c] = _sort_key(score_of(kh, kl))
        return 0

    lax.fori_loop(0, n_chunks, idx_body, 0)
    ikn = ikn_ref[...]
    ikn8 = jnp.concatenate([ikn, jnp.zeros((SUBLANES - 1, IDX_DIM), F32)], axis=0)
    nh = ikn8.astype(BF16)
    nl = (ikn8 - nh.astype(F32)).astype(BF16)
    s_new = score_of(nh, nl)[:, 0:1]
    lane = lax.broadcasted_iota(I32, (1, kc), 1)
    key_ref[n_chunks] = _sort_key(jnp.where(lane == 0, s_new, -jnp.inf))

    t, jcut = _topk_threshold(key_ref, n_chunks + 1, kc, 1, k_sel, idx_bits)

    def selected(c):
        kk = key_ref[c]
        pos = c * kc + lane
        return ((kk > t) | ((kk == t) & (pos <= jcut))) & (pos <= past)

    q = qb_ref[...].astype(F32)
    rows = []
    for h in range(N_HEADS):
        g = h // 2
        tile = q[:, g * LANES:(g + 1) * LANES]
        src = tile if (g % 2) == (h % 2) else pltpu.roll(tile, HEAD_DIM, 1)
        lo = (g % 2) * HEAD_DIM
        lane_t = lax.broadcasted_iota(I32, (1, LANES), 1)
        placed = jnp.where((lane_t >= lo) & (lane_t < lo + HEAD_DIM), src, 0.0)
        zero = jnp.zeros((1, LANES), F32)
        rows.append(jnp.concatenate([placed, zero] if g < 2 else [zero, placed], axis=1))
    qrows = jnp.concatenate(rows, axis=0)
    lo_lane = (lax.broadcasted_iota(I32, (N_HEADS, KV_WIDTH), 0) // 2) * HEAD_DIM
    lane8 = lax.broadcasted_iota(I32, (N_HEADS, KV_WIDTH), 1)
    keep = (lane8 >= lo_lane) & (lane8 < lo_lane + HEAD_DIM)
    qrows_b = qrows.astype(BF16)

    def att_body(c, carry):
        m_old, l_old, acc = carry

        @pl.when(c + 1 < n_chunks)
        def _():
            start_kv(c + 1)

        wait_kv(c)
        slot = c % 2
        kblk = kbuf[slot].reshape(kc, KV_WIDTH).astype(BF16)
        vblk = vbuf[slot].reshape(kc, KV_WIDTH).astype(BF16)
        s = jnp.where(selected(c), _dot_nt(qrows_b, kblk), NEG_BIG)
        m_new = jnp.maximum(m_old, jnp.max(s, axis=1, keepdims=True))
        alpha = jnp.exp(m_old - m_new)
        p = jnp.exp(s - m_new)
        l_new = alpha * l_old + jnp.sum(p, axis=1, keepdims=True)
        acc = alpha * acc + _dot(p.astype(BF16), vblk)
        return m_new, l_new, acc

    init = (jnp.full((N_HEADS, 1), NEG_BIG, F32), jnp.zeros((N_HEADS, 1), F32),
            jnp.zeros((N_HEADS, KV_WIDTH), F32))
    m_old, l_old, acc = lax.fori_loop(0, n_chunks, att_body, init)

    kn = kn_ref[...].astype(F32)
    vn = vn_ref[...].astype(F32)
    sel_new = selected(n_chunks)[:, 0:1]
    s = jnp.where(sel_new, jnp.sum(qrows_b.astype(F32) * kn, axis=1, keepdims=True), NEG_BIG)
    m_new = jnp.maximum(m_old, s)
    alpha = jnp.exp(m_old - m_new)
    p = jnp.exp(s - m_new)
    l_new = alpha * l_old + p
    acc = alpha * acc + p.astype(BF16).astype(F32) * vn
    res = jnp.where(keep, acc / l_new, 0.0)
    out_r = []
    for r in range(2):
        row = res[r:r + 1, :]
        for g in range(1, N_KV_HEADS):
            row = row + res[2 * g + r:2 * g + r + 1, :]
        out_r.append(row)
    o_ref[...] = jnp.concatenate(out_r, axis=1)


def _dsa_sample(page_table, qb, iqh, iql, h, kb, vb, iko, cache_k, cache_v, cache_ik, layer):
    ns, n_pages = page_table.shape
    depth, n_pool = cache_k.shape[0], cache_k.shape[1]
    total = n_pages * PAGE_SIZE + 1
    k_sel = min(TOPK_MAX, total // 4)
    idx_bits = max(1, (total - 1).bit_length())
    n_chunks = n_pages // PAGES_PER_CHUNK

    def row(width):
        return pl.BlockSpec((None, 1, width), lambda b, pt: (b, 0, 0))

    any_spec = pl.BlockSpec(memory_space=pl.ANY)
    ikw = h[:, OFF_IKW:OFF_IKW + LANES]
    grid_spec = pltpu.PrefetchScalarGridSpec(
        num_scalar_prefetch=1,
        grid=(ns,),
        in_specs=[row(ATT_WIDTH), row(KV_WIDTH), row(KV_WIDTH), row(LANES), row(KV_WIDTH), row(KV_WIDTH),
                  row(IDX_DIM), any_spec, any_spec, any_spec],
        out_specs=row(ATT_WIDTH),
        scratch_shapes=[
            pltpu.VMEM((n_pages, PAGE_SIZE, IDX_DIM), F32),
            pltpu.VMEM((2, PAGES_PER_CHUNK, PAGE_SIZE, KV_WIDTH), F32),
            pltpu.VMEM((2, PAGES_PER_CHUNK, PAGE_SIZE, KV_WIDTH), F32),
            pltpu.VMEM((n_chunks + 1, 1, SAMPLE_KC), I32),
            pltpu.SemaphoreType.DMA(()),
            pltpu.SemaphoreType.DMA((2, 2)),
        ],
    )
    r3 = lambda a: a.reshape(ns, 1, a.shape[-1])
    out = pl.pallas_call(
        functools.partial(_dsa_sample_kernel, n_pages=n_pages, page_base=layer * n_pool, k_sel=k_sel,
                          idx_bits=idx_bits),
        grid_spec=grid_spec,
        out_shape=jax.ShapeDtypeStruct((ns, 1, ATT_WIDTH), F32),
        compiler_params=_cparams("arbitrary"),
        name="dsa_sample",
    )(page_table, r3(qb), r3(iqh), r3(iql), r3(ikw), r3(kb), r3(vb), r3(iko),
      cache_k.reshape(depth * n_pool, PAGE_SIZE, KV_WIDTH), cache_v.reshape(depth * n_pool, PAGE_SIZE, KV_WIDTH),
      cache_ik.reshape(depth * n_pool, PAGE_SIZE, IDX_DIM))
    return out.reshape(ns, ATT_WIDTH)


def _complex_pow(re, im, n):
    rr, ri = jnp.ones_like(re), jnp.zeros_like(im)
    while n:
        if n & 1:
            rr, ri = rr * re - ri * im, rr * im + ri * re
        re, im = re * re - im * im, 2.0 * re * im
        n >>= 1
    return rr, ri


def _layer_params(lw, lseg):
    w = lw["w_in"]
    offs = [0]
    for s in IN_SIZES:
        offs.append(offs[-1] + s)
    u_, q_, k_, v_, iq_, ik_, iw_, z_, xbc_, dt_, gt_ = [w[:, offs[i]:offs[i + 1]] for i in range(len(IN_SIZES))]
    zpad = lambda n: jnp.zeros((D_MODEL, n), w.dtype)
    wp = jnp.concatenate([gt_, z_, u_, xbc_, q_, k_, v_, iq_, ik_, iw_, zpad(LANES - IDX_DIM - IDX_HEADS),
                          dt_, zpad(LANES - SSM_HEADS)], axis=1).astype(BF16)
    p = {"wp": wp, "norm_mix": lw["norm_mix"].reshape(1, D_MODEL)}
    p["qg"] = jnp.tile(lw["q_norm"], N_HEADS).reshape(1, ATT_WIDTH)
    p["kg"] = jnp.tile(lw["k_norm"], N_KV_HEADS).reshape(1, KV_WIDTH)
    ones = jnp.ones((HEAD_DIM, HEAD_DIM), BF16)
    p["bdq"] = jnp.kron(jnp.eye(N_HEADS, dtype=BF16), ones)
    p["bdk"] = jnp.kron(jnp.eye(N_KV_HEADS, dtype=BF16), ones)

    ar, ai = lw["s5_a_re"], lw["s5_a_im"]
    dt = jnp.exp(lw["s5_log_dt"])[:, None]
    mag = jnp.exp(dt * ar)
    abar_re, abar_im = mag * jnp.cos(dt * ai), mag * jnp.sin(dt * ai)
    den = ar * ar + ai * ai
    nr, ni = abar_re - 1.0, abar_im
    coef_re = (nr * ar + ni * ai) / den
    coef_im = (ni * ar - nr * ai) / den
    bc_re = coef_re[:, :, None] * lw["s5_b_re"] - coef_im[:, :, None] * lw["s5_b_im"]
    bc_im = coef_re[:, :, None] * lw["s5_b_im"] + coef_im[:, :, None] * lw["s5_b_re"]
    eye = jnp.eye(S5_GROUPS, dtype=F32)
    embed_b = lambda b: jnp.einsum("gnc,gh->gchn", b, eye).reshape(S5_WIDTH, S5_LANES)
    embed_c = lambda c: jnp.einsum("gcn,gh->gnhc", c, eye).reshape(S5_LANES, S5_WIDTH)
    p["bmat"] = jnp.concatenate([embed_b(bc_re), embed_b(bc_im)], axis=1).astype(BF16)
    p["cmat"] = jnp.concatenate([embed_c(lw["s5_c_re"]), -embed_c(lw["s5_c_im"])], axis=0).astype(BF16)
    p["ar"] = abar_re.reshape(1, S5_LANES)
    p["ai"] = abar_im.reshape(1, S5_LANES)
    pr, pi = _complex_pow(abar_re, abar_im, lseg)
    p["pr"] = pr.reshape(1, S5_LANES)
    p["pi"] = pi.reshape(1, S5_LANES)
    p["dskip"] = lw["s5_d"].reshape(1, S5_WIDTH)
    p["wglu"] = lw["s5_w_glu"].astype(BF16)

    p["cw"] = lw["conv_w"]
    p["cb"] = lw["conv_b"].reshape(1, CONV_DIM)
    padh = lambda v: jnp.pad(v, (0, LANES - SSM_HEADS)).reshape(1, LANES)
    p["dtb"] = padh(lw["dt_bias"])
    p["alog"] = padh(lw["a_log"])
    p["dexp"] = jnp.repeat(lw["ssm_d"], SSM_HEAD_DIM).reshape(1, SSM_INNER)
    p["nrm"] = lw["ssm_norm"].reshape(1, SSM_INNER)
    p["expand"] = jnp.pad(jnp.kron(jnp.eye(SSM_HEADS, dtype=F32), jnp.ones((1, SSM_HEAD_DIM), F32)),
                          ((0, LANES - SSM_HEADS), (0, 0))).astype(BF16)

    p["w1"] = lw["w_br_s5"].astype(BF16)
    p["w2"] = (lw["w_br_att"].reshape(N_KV_HEADS, 2, HEAD_DIM, D_MODEL).transpose(1, 0, 2, 3)
               .reshape(ATT_WIDTH, D_MODEL).astype(BF16))
    p["w3"] = lw["w_br_ssm"].astype(BF16)
    p["wo"] = lw["w_o"].astype(BF16)
    p["norm_mlp"] = lw["norm_mlp"].reshape(1, D_MODEL)
    p["wu"] = lw["w_up"].astype(BF16)
    p["wd"] = lw["w_down"].astype(BF16)
    return p


def _rope_table(pos):
    half = ROT_DIM // 2
    inv_freq = ROPE_THETA ** (-jnp.arange(half, dtype=F32) / half)
    ang = pos.astype(F32)[:, None] * inv_freq[None, :]
    cos, sin = jnp.cos(ang), jnp.sin(ang)
    n = pos.shape[0]
    rest = HEAD_DIM - ROT_DIM
    c = jnp.concatenate([cos, cos, jnp.ones((n, rest), F32)], axis=1)
    sa = jnp.concatenate([-sin, jnp.zeros((n, half + rest), F32)], axis=1)
    sb = jnp.concatenate([jnp.zeros((n, half), F32), sin, jnp.zeros((n, rest), F32)], axis=1)
    rep = LANES // HEAD_DIM
    return jnp.concatenate([jnp.tile(c, (1, rep)), jnp.tile(sa, (1, rep)), jnp.tile(sb, (1, rep))], axis=1)


def _pick(n, pref):
    for t in pref:
        if n % t == 0:
            return t
    return n


def _layer_prompt(x2, p, tab, nb, lt):
    m = nb * lt
    lseg = lt // SUBLANES
    h = _inproj(x2, p["norm_mix"], p["wp"], _pick(m, (1024, 512, 256, 128)))
    qb, ko, kb, vo, vb, iqh, iql, iko, ikh, ikl = _post(
        h, tab, p["qg"], p["kg"], p["bdq"], p["bdk"], nb, lt, _pick(lt, (512, 256, 128)))
    oatt = _dsa_prompt(qb, iqh, iql, h, kb, vb, ikh, ikl, nb, lt)

    u4 = h.reshape(nb, SUBLANES, lseg, NP)
    tt = _pick(lseg, (64, 32, 16, 8))
    ends = _s5_ends(u4, OFF_U // S5_WIDTH, p["bmat"], p["ar"], p["ai"], tt)
    h0 = _s5_carry(ends, p["pr"], p["pi"])
    os5, hl = _s5_main(u4, OFF_U // S5_WIDTH, h0, p["bmat"], p["cmat"], p["ar"], p["ai"], p["dskip"],
                       p["wglu"], tt)

    ossm, st = _ssd_prompt(h, p["cw"], p["cb"], p["dtb"], p["alog"], p["dexp"], p["nrm"], p["expand"], nb, lt)

    tm = _pick(m, (512, 256, 128))
    x1 = _merge(x2, h, os5.reshape(m, S5_WIDTH), oatt, ossm, p["w1"], p["w2"], p["w3"], p["wo"], tm)
    xo = _mlp(x1, p["norm_mlp"], p["wu"], p["wd"], _pick(m, (256, 128)))

    new_conv = h.reshape(nb, lt, NP)[:, lt - (CONV_WIDTH - 1):, OFF_XBC:OFF_XBC + CONV_DIM]
    state = (ko.reshape(nb, lt, N_KV_HEADS, HEAD_DIM), vo.reshape(nb, lt, N_KV_HEADS, HEAD_DIM),
             iko.reshape(nb, lt, IDX_DIM),
             hl[:, SUBLANES - 1, 0:S5_LANES].reshape(nb, S5_GROUPS, S5_STATE),
             hl[:, SUBLANES - 1, S5_LANES:].reshape(nb, S5_GROUPS, S5_STATE),
             new_conv, st.reshape(nb, SSM_HEADS, SSM_HEAD_DIM, SSM_STATE))
    return xo, state


def _layer_sample(x2, p, tab, caches, states):
    cache_k, cache_v, cache_ik, page_table, layer = caches
    s5_re, s5_im, conv0, ssm0 = states
    ns = x2.shape[0]
    ng = ns // SUBLANES
    h = _inproj(x2, p["norm_mix"], p["wp"], ns)
    qb, ko, kb, vo, vb, iqh, iql, iko, ikh, ikl = _post(
        h, tab, p["qg"], p["kg"], p["bdq"], p["bdk"], 1, ns, ns)
    oatt = _dsa_sample(page_table, qb, iqh, iql, h, kb, vb, iko, cache_k, cache_v, cache_ik, layer)

    u4 = h.reshape(ng, SUBLANES, 1, NP)
    h0 = jnp.concatenate([s5_re.reshape(ng, SUBLANES, S5_LANES), s5_im.reshape(ng, SUBLANES, S5_LANES)], axis=2)
    os5, hl = _s5_main(u4, OFF_U // S5_WIDTH, h0, p["bmat"], p["cmat"], p["ar"], p["ai"], p["dskip"],
                       p["wglu"], 1)

    ossm, cst, st = _ssd_step(h, conv0.transpose(1, 0, 2), ssm0.reshape(ns, SSM_INNER, SSM_STATE),
                              p["cw"], p["cb"], p["dtb"], p["alog"], p["dexp"], p["nrm"], p["expand"])

    x1 = _merge(x2, h, os5.reshape(ns, S5_WIDTH), oatt, ossm, p["w1"], p["w2"], p["w3"], p["wo"], ns)
    xo = _mlp(x1, p["norm_mlp"], p["wu"], p["wd"], ns)
    state = (ko.reshape(ns, 1, N_KV_HEADS, HEAD_DIM), vo.reshape(ns, 1, N_KV_HEADS, HEAD_DIM),
             iko.reshape(ns, 1, IDX_DIM),
             hl[:, :, 0:S5_LANES].reshape(ns, S5_GROUPS, S5_STATE),
             hl[:, :, S5_LANES:].reshape(ns, S5_GROUPS, S5_STATE),
             cst.transpose(1, 0, 2), st.reshape(ns, SSM_HEADS, SSM_HEAD_DIM, SSM_STATE))
    return xo, state


_WEIGHTS =("norm_mix", "w_in", "q_norm", "k_norm", "s5_a_re", "s5_a_im", "s5_log_dt", "s5_b_re", "s5_b_im",
            "s5_c_re", "s5_c_im", "s5_d", "s5_w_glu", "conv_w", "conv_b", "dt_bias", "a_log", "ssm_d", "ssm_norm",
            "w_br_s5", "w_br_att", "w_br_ssm", "w_o", "norm_mlp", "w_up", "w_down")


def kernel(x_prompt, x_sample, cache_k, cache_v, cache_idx_k, state_s5_re, state_s5_im, state_conv, state_ssm, page_table, norm_mix, w_in, q_norm, k_norm, s5_a_re, s5_a_im, s5_log_dt, s5_b_re, s5_b_im, s5_c_re, s5_c_im, s5_d, s5_w_glu, conv_w, conv_b, dt_bias, a_log, ssm_d, ssm_norm, w_br_s5, w_br_att, w_br_ssm, w_o, norm_mlp, w_up, w_down):
    weights = dict(zip(_WEIGHTS, (norm_mix, w_in, q_norm, k_norm, s5_a_re, s5_a_im, s5_log_dt, s5_b_re, s5_b_im,
                                  s5_c_re, s5_c_im, s5_d, s5_w_glu, conv_w, conv_b, dt_bias, a_log, ssm_d, ssm_norm,
                                  w_br_s5, w_br_att, w_br_ssm, w_o, norm_mlp, w_up, w_down)))
    depth = w_in.shape[0]
    nb, lt, _ = x_prompt.shape
    ns, n_new, _ = x_sample.shape
    assert n_new == 1 and ns % SUBLANES == 0 and lt % (SUBLANES * Q_BLOCK) == 0
    past = page_table.shape[1] * PAGE_SIZE
    tab_p = _rope_table(jnp.arange(lt))
    tab_s = _rope_table(jnp.full((ns,), past, jnp.int32))
    xp = x_prompt.reshape(nb * lt, D_MODEL)
    xs = x_sample.reshape(ns, D_MODEL)
    st_p, st_s = [], []
    for l in range(depth):
        p = _layer_params({k: v[l] for k, v in weights.items()}, lt // SUBLANES)
        xp, sp = _layer_prompt(xp, p, tab_p, nb, lt)
        caches = (cache_k, cache_v, cache_idx_k, page_table, l)
        states = (state_s5_re[l], state_s5_im[l], state_conv[l], state_ssm[l])
        xs, ss = _layer_sample(xs, p, tab_s, caches, states)
        st_p.append(sp)
        st_s.append(ss)
    outs_p = [jnp.stack([s[i] for s in st_p]) for i in range(7)]
    outs_s = [jnp.stack([s[i] for s in st_s]) for i in range(7)]
    return (xp.reshape(nb, lt, D_MODEL), xs.reshape(ns, 1, D_MODEL), *outs_p, *outs_s)
```

```python
import functools
import math

import jax
import jax.numpy as jnp
from jax import lax
from jax.experimental import pallas as pl
from jax.experimental.pallas import tpu as pltpu

F32 = jnp.float32
BF16 = jnp.bfloat16
I32 = jnp.int32

D_MODEL = 1024
EPS = 1e-6
S5_GROUP = 16
S5_WIDTH = 512
S5_GROUPS = 32
S5_STATE = 64
S5_LANES = S5_GROUPS * S5_STATE
N_HEADS = 8
N_KV_HEADS = 4
HEAD_DIM = 64
ATT_WIDTH = 512
KV_WIDTH = 256
ROT_DIM = 16
ROPE_THETA = 500000.0
IDX_HEADS = 4
IDX_DIM = 64
IDX_SCALE = (IDX_HEADS * IDX_DIM) ** -0.5
TOPK_MAX = 256
Q_BLOCK = 128
PAGE_SIZE = 128
SSM_INNER = 1024
SSM_HEAD_DIM = 64
SSM_HEADS = 16
SSM_GROUPS = 2
SSM_STATE = 128
CONV_WIDTH = 4
CONV_DIM = 1536
SSD_CHUNK = 128
D_FF = 4096
IN_SIZES = (512, 512, 256, 256, 256, 64, 4, 1024, 1536, 16, 3072)

LANES = 128
SUBLANES = 8
VMEM_LIMIT = 56 * 1024 * 1024

NP = 7680
OFF_GATES, OFF_Z, OFF_U, OFF_XBC, OFF_Q, OFF_K, OFF_V, OFF_IQ, OFF_IKW, OFF_DT = (
    0, 3072, 4096, 4608, 6144, 6656, 6912, 7168, 7424, 7552)

NEG_BIG = -1e30
Q_SCALE = HEAD_DIM ** -0.5 * math.log2(math.e)
V_ROWS = HEAD_DIM + 16
INT_MIN = -(2 ** 31)


def _cparams(*sem):
    return pltpu.CompilerParams(dimension_semantics=sem, vmem_limit_bytes=VMEM_LIMIT)


def _split3(x):
    hi = x.astype(BF16)
    r = x - hi.astype(F32)
    mid = r.astype(BF16)
    lo = (r - mid.astype(F32)).astype(BF16)
    return hi, mid, lo


def _dot(a, b):
    return jnp.dot(a, b, preferred_element_type=F32)


def _dot_nt(a, b):
    return lax.dot_general(a, b, (((1,), (1,)), ((), ())), preferred_element_type=F32)


def _dot_tn(a, b):
    return lax.dot_general(a, b, (((0,), (0,)), ((), ())), preferred_element_type=F32)


def _dot3_exact_rhs(x, m_bf16):
    hi, mid, lo = _split3(x)
    return _dot(hi, m_bf16) + _dot(mid, m_bf16) + _dot(lo, m_bf16)


def _dot3_exact_lhs(m_bf16, x):
    hi, mid, lo = _split3(x)
    return _dot(m_bf16, hi) + _dot(m_bf16, mid) + _dot(m_bf16, lo)


def _inproj_kernel(x_ref, g_ref, w_ref, o_ref, xn_ref):
    @pl.when(pl.program_id(1) == 0)
    def _():
        x = x_ref[...]
        ms = jnp.mean(x * x, axis=-1, keepdims=True)
        xn_ref[...] = (x * lax.rsqrt(ms + EPS) * g_ref[...]).astype(BF16)

    o_ref[...] = _dot(xn_ref[...], w_ref[...])


def _inproj(x, g, wp, tm, tn=1536):
    m = x.shape[0]
    return pl.pallas_call(
        _inproj_kernel,
        grid=(m // tm, NP // tn),
        in_specs=[
            pl.BlockSpec((tm, D_MODEL), lambda i, j: (i, 0)),
            pl.BlockSpec((1, D_MODEL), lambda i, j: (0, 0)),
            pl.BlockSpec((D_MODEL, tn), lambda i, j: (0, j)),
        ],
        out_specs=pl.BlockSpec((tm, tn), lambda i, j: (i, j)),
        out_shape=jax.ShapeDtypeStruct((m, NP), F32),
        scratch_shapes=[pltpu.VMEM((tm, D_MODEL), BF16)],
        compiler_params=_cparams("parallel", "arbitrary"),
        name="inproj",
    )(x, g, wp)


def _rope(x, c, sa, sb):
    outs = []
    for t in range(x.shape[1] // LANES):
        xt = x[:, t * LANES:(t + 1) * LANES]
        outs.append(xt * c + pltpu.roll(xt, LANES - ROT_DIM // 2, 1) * sa + pltpu.roll(xt, ROT_DIM // 2, 1) * sb)
    return outs[0] if len(outs) == 1 else jnp.concatenate(outs, axis=1)


def _head_rms(x, bd_ref, gain):
    ms = _dot3_exact_rhs(x * x, bd_ref[...]) * (1.0 / HEAD_DIM)
    return x * lax.rsqrt(ms + EPS) * gain


def _post_kernel(q_ref, k_ref, v_ref, iq_ref, ikw_ref, tab_ref, qg_ref, kg_ref, bdq_ref, bdk_ref,
                 qb_ref, ko_ref, kb_ref, vo_ref, vb_ref, iqh_ref, iql_ref, iko_ref, ikc_ref,
                 *, v_transposed):
    c = tab_ref[:, 0:LANES]
    sa = tab_ref[:, LANES:2 * LANES]
    sb = tab_ref[:, 2 * LANES:3 * LANES]
    q = _rope(_head_rms(q_ref[...], bdq_ref, qg_ref[...]), c, sa, sb)
    qb_ref[...] = (q * Q_SCALE).astype(BF16)
    k = _rope(_head_rms(k_ref[...], bdk_ref, kg_ref[...]), c, sa, sb)
    ko_ref[...] = k
    kb_ref[...] = k.astype(BF16)
    v = v_ref[...]
    vo_ref[...] = v
    if v_transposed:
        v_t = v.T.astype(BF16)
        ones = jnp.ones((V_ROWS - HEAD_DIM, v.shape[0]), BF16)
        vb_ref[...] = jnp.concatenate(
            [blk for g in range(N_KV_HEADS) for blk in (v_t[g * HEAD_DIM:(g + 1) * HEAD_DIM, :], ones)], axis=0)
    else:
        vb_ref[...] = v.astype(BF16)
    iq = _rope(iq_ref[...], c, sa, sb)
    iqh = iq.astype(BF16)
    iqh_ref[...] = iqh
    iql_ref[...] = (iq - iqh.astype(F32)).astype(BF16)
    ikr = _rope(ikw_ref[...], c, sa, sb)
    iko_ref[...] = ikr[:, 0:IDX_DIM]
    lane = lax.broadcasted_iota(I32, ikr.shape, 1)
    ik = jnp.where(lane < IDX_DIM, ikr, 0.0)
    ikh = ik.astype(BF16)
    ikl = pltpu.roll(ik - ikh.astype(F32), IDX_DIM, 1).astype(BF16)
    ikc_ref[...] = jnp.concatenate([ikh + ikl, ikh], axis=1)


def _post(h, tab, qg, kg, bdq, bdk, nb, lt, tt, v_transposed):
    m = nb * lt
    nt = lt // tt

    def col(width, off):
        return pl.BlockSpec((tt, width), lambda b, i: (b * nt + i, off // width))

    def out(width):
        return pl.BlockSpec((tt, width), lambda b, i: (b * nt + i, 0))

    def full(shape):
        return pl.BlockSpec(shape, lambda b, i: (0, 0))

    shapes = [
        (ATT_WIDTH, BF16), (KV_WIDTH, F32), (KV_WIDTH, BF16), (KV_WIDTH, F32), (KV_WIDTH, BF16),
        (KV_WIDTH, BF16), (KV_WIDTH, BF16), (IDX_DIM, F32), (KV_WIDTH, BF16)]
    out_specs = [out(w) for w, _ in shapes]
    out_shape = [jax.ShapeDtypeStruct((m, w), d) for w, d in shapes]
    if v_transposed:
        out_specs[4] = pl.BlockSpec((None, N_KV_HEADS * V_ROWS, tt), lambda b, i: (b * nt + i, 0, 0))
        out_shape[4] = jax.ShapeDtypeStruct((m // tt, N_KV_HEADS * V_ROWS, tt), BF16)
    return pl.pallas_call(
        functools.partial(_post_kernel, v_transposed=v_transposed),
        grid=(nb, nt),
        in_specs=[
            col(ATT_WIDTH, OFF_Q), col(KV_WIDTH, OFF_K), col(KV_WIDTH, OFF_V), col(KV_WIDTH, OFF_IQ),
            col(LANES, OFF_IKW),
            pl.BlockSpec((tt, 3 * LANES), lambda b, i: (i, 0)),
            full((1, ATT_WIDTH)), full((1, KV_WIDTH)), full((ATT_WIDTH, ATT_WIDTH)), full((KV_WIDTH, KV_WIDTH)),
        ],
        out_specs=out_specs,
        out_shape=out_shape,
        compiler_params=_cparams("parallel", "parallel"),
        name="qk_post",
    )(h, h, h, h, h, tab, qg, kg, bdq, bdk)


def _sort_key(score):
    score = jnp.where(score == 0.0, 0.0, score)
    bits = pltpu.bitcast(score, I32)
    return bits ^ (lax.shift_right_arithmetic(bits, 31) & 0x7FFFFFFF)


def _lane_fold(x):
    acc = x[:, 0:LANES]
    for t in range(1, x.shape[1] // LANES):
        acc = acc + x[:, t * LANES:(t + 1) * LANES]
    return acc


def _topk_threshold(key_ref, nkc, kc, rows, k_sel, idx_bits):
    def count(pred):
        def body(c, acc):
            return acc + _lane_fold(pred(key_ref[c], c).astype(I32))
        acc = lax.fori_loop(0, nkc, body, jnp.zeros((rows, LANES), I32))
        return jnp.sum(acc, axis=1, keepdims=True)

    def bit_body(i, carry):
        t_u, cnt_t = carry
        cand_u = t_u | lax.shift_left(jnp.int32(1), 31 - i)
        cand_s = cand_u ^ INT_MIN
        cnt = count(lambda kk, c: kk >= cand_s)
        take = cnt >= k_sel
        return jnp.where(take, cand_u, t_u), jnp.where(take, cnt, cnt_t)

    t_u0 = jnp.zeros((rows, 1), I32)
    cnt0 = jnp.full((rows, 1), k_sel, I32)
    t_u, cnt_t = lax.fori_loop(0, 32, bit_body, (t_u0, cnt0))
    t = t_u ^ INT_MIN
    has_ties = jnp.max(jnp.where(cnt_t > k_sel, 1, 0)) > 0

    def tie_search(_):
        need_m1 = k_sel - 1 - count(lambda kk, c: kk > t)

        def pos_of(c):
            return c * kc + lax.broadcasted_iota(I32, (rows, kc), 1)

        def jbody(i, j):
            cand = j | lax.shift_left(jnp.int32(1), idx_bits - 1 - i)
            f = count(lambda kk, c: (kk == t) & (pos_of(c) < cand))
            return jnp.where(f <= need_m1, cand, j)

        return lax.fori_loop(0, idx_bits, jbody, jnp.zeros((rows, 1), I32))

    j = lax.cond(has_ties, tie_search, lambda _: jnp.full((rows, 1), 2 ** 30, I32), 0)
    return t, j


NEG_INF_KEY = -2139095041


def _topk_threshold_t(key_ref, nkc, kc, k_sel, idx_bits):
    def count(pred):
        def body(c, acc):
            hit = pred(key_ref[c], c).astype(I32)
            return acc + jnp.sum(hit.reshape(kc // SUBLANES, SUBLANES, LANES), axis=0)
        acc = lax.fori_loop(0, nkc, body, jnp.zeros((SUBLANES, LANES), I32))
        return jnp.sum(acc, axis=0, keepdims=True)

    def bit_body(i, carry):
        t_u, cnt_t = carry
        cand_u = t_u | lax.shift_left(jnp.int32(1), 31 - i)
        cand_s = cand_u ^ INT_MIN
        cnt = count(lambda kk, c: kk >= cand_s)
        take = cnt >= k_sel
        return jnp.where(take, cand_u, t_u), jnp.where(take, cnt, cnt_t)

    t_u0 = jnp.zeros((1, LANES), I32)
    cnt0 = jnp.full((1, LANES), 1, I32) * (nkc * kc)
    t_u, cnt_t = lax.fori_loop(0, 32, bit_body, (t_u0, cnt0))
    t = t_u ^ INT_MIN

    @pl.when(jnp.max(cnt_t) > k_sel)
    def _():
        need_m1 = k_sel - 1 - count(lambda kk, c: kk > t)

        def pos_of(c):
            return c * kc + lax.broadcasted_iota(I32, (kc, LANES), 0)

        def jbody(i, j):
            cand = j | lax.shift_left(jnp.int32(1), idx_bits - 1 - i)
            f = count(lambda kk, c: (kk == t) & (pos_of(c) < cand))
            return jnp.where(f <= need_m1, cand, j)

        jcut = lax.fori_loop(0, idx_bits, jbody, jnp.zeros((1, LANES), I32))

        def demote(c, _):
            kk = key_ref[c]
            key_ref[c] = jnp.where((kk == t) & (pos_of(c) > jcut), t - 1, kk)
            return 0

        lax.fori_loop(0, nkc, demote, 0)

    return jnp.maximum(t, NEG_INF_KEY + 1)


def _dsa_kernel(qb_ref, iqh_ref, iql_ref, ikw_ref, kb_ref, vt_ref, ikc_ref, o_ref,
                key_ref, wq_ref, wi_ref, s_ref, bias_ref, m_ref, acc_ref, *, kc, k_sel, idx_bits):
    qi = pl.program_id(1)
    nkc = (qi * Q_BLOCK) // kc + 1
    qpos = qi * Q_BLOCK + lax.broadcasted_iota(I32, (1, Q_BLOCK), 1)

    halves = []
    for ref in (iqh_ref, iql_ref):
        iq_t = ref[...].astype(F32).T
        halves.append(jnp.concatenate(
            [iq_t[h * IDX_DIM:(h + 1) * IDX_DIM, :] for h in range(IDX_HEADS)], axis=1).astype(BF16))
    wi_ref[...] = jnp.concatenate([halves[0], halves[0], halves[1], jnp.zeros_like(halves[0])], axis=0)
    iw_t = ikw_ref[...].T
    q_t = qb_ref[...].astype(F32).T
    zrow = jnp.zeros((HEAD_DIM, 2 * Q_BLOCK), F32)
    for g in range(N_KV_HEADS):
        blk = jnp.concatenate([q_t[(2 * g + r) * HEAD_DIM:(2 * g + r + 1) * HEAD_DIM, :] for r in range(2)], axis=1)
        rows = [blk if gg == g else zrow for gg in range(N_KV_HEADS)]
        wq_ref[g] = jnp.concatenate(rows, axis=0).astype(BF16)

    def idx_body(c, _):
        r0 = pl.multiple_of(c * kc, kc)
        d = jnp.maximum(_dot(ikc_ref[pl.ds(r0, kc), :], wi_ref[...]), 0.0)
        sc = iw_t[IDX_DIM:IDX_DIM + 1, :] * d[:, 0:Q_BLOCK]
        for h in range(1, IDX_HEADS):
            sc = sc + iw_t[IDX_DIM + h:IDX_DIM + h + 1, :] * d[:, h * Q_BLOCK:(h + 1) * Q_BLOCK]
        kpos = c * kc + lax.broadcasted_iota(I32, (kc, Q_BLOCK), 0)
        sc = jnp.where(kpos <= qpos, sc * IDX_SCALE, -jnp.inf)
        key_ref[c] = _sort_key(sc)
        return 0

    lax.fori_loop(0, nkc, idx_body, 0)
    t = _topk_threshold_t(key_ref, nkc, kc, k_sel, idx_bits)

    m_ref[...] = jnp.full(m_ref.shape, NEG_BIG, F32)
    acc_ref[...] = jnp.zeros(acc_ref.shape, F32)

    def att_body(c, _):
        r0 = pl.multiple_of(c * kc, kc)
        kblk = kb_ref[pl.ds(r0, kc), :]
        bias_ref[...] = jnp.where(key_ref[c] >= t, 0.0, NEG_BIG)
        alphas = []
        for g in range(N_KV_HEADS):
            s = _dot(kblk, wq_ref[g])
            cmax = []
            for r in range(2):
                sr = s[:, r * Q_BLOCK:(r + 1) * Q_BLOCK] + bias_ref[...]
                s_ref[g, :, r * Q_BLOCK:(r + 1) * Q_BLOCK] = sr
                cmax.append(jnp.max(sr, axis=0, keepdims=True))
            m_old = m_ref[g]
            m_new = jnp.maximum(m_old, jnp.concatenate(cmax, axis=1))
            m_ref[g] = m_new
            alphas.append(jnp.exp2(m_old - m_new))
        for g in range(N_KV_HEADS):
            p = jnp.exp2(s_ref[g] - m_ref[g]).astype(BF16)
            vt = vt_ref[c, g * V_ROWS:(g + 1) * V_ROWS, :]
            acc_ref[g] = alphas[g] * acc_ref[g] + _dot(vt, p)
        return 0

    lax.fori_loop(0, nkc, att_body, 0)

    res_t = jnp.concatenate(
        [acc_ref[g, 0:HEAD_DIM, :] / acc_ref[g, HEAD_DIM:HEAD_DIM + 1, :] for g in range(N_KV_HEADS)],
        axis=0)
    res = res_t.T
    o_ref[...] = jnp.concatenate([res[0:Q_BLOCK], res[Q_BLOCK:]], axis=1)


def _dsa_prompt(qb, iqh, iql, h, kb, vt, ikc, nb, lt, kc):
    nq = lt // Q_BLOCK
    nck = lt // kc
    k_sel = min(TOPK_MAX, lt // 4)
    idx_bits = max(1, (lt - 1).bit_length())

    def qblk(width, off=0):
        return pl.BlockSpec((Q_BLOCK, width), lambda b, i: (b * nq + i, off // width))

    def seq(width):
        return pl.BlockSpec((lt, width), lambda b, i: (b, 0))

    return pl.pallas_call(
        functools.partial(_dsa_kernel, kc=kc, k_sel=k_sel, idx_bits=idx_bits),
        grid=(nb, nq),
        in_specs=[qblk(ATT_WIDTH), qblk(KV_WIDTH), qblk(KV_WIDTH), qblk(LANES, OFF_IKW),
                  seq(KV_WIDTH), pl.BlockSpec((nck, N_KV_HEADS * V_ROWS, kc), lambda b, i: (b, 0, 0)),
                  seq(KV_WIDTH)],
        out_specs=qblk(ATT_WIDTH),
        out_shape=jax.ShapeDtypeStruct((nb * lt, ATT_WIDTH), F32),
        scratch_shapes=[
            pltpu.VMEM((nck, kc, Q_BLOCK), I32),
            pltpu.VMEM((N_KV_HEADS, KV_WIDTH, 2 * Q_BLOCK), BF16),
            pltpu.VMEM((4 * IDX_DIM, IDX_HEADS * Q_BLOCK), BF16),
            pltpu.VMEM((N_KV_HEADS, kc, 2 * Q_BLOCK), F32),
            pltpu.VMEM((kc, Q_BLOCK), F32),
            pltpu.VMEM((N_KV_HEADS, 1, 2 * Q_BLOCK), F32),
            pltpu.VMEM((N_KV_HEADS, V_ROWS, 2 * Q_BLOCK), F32),
        ],
        compiler_params=_cparams("parallel", "arbitrary"),
        name="dsa_prompt",
    )(qb, iqh, iql, h, kb, vt, ikc)


S5_LANE_BLOCK = 512


S5_TILES = S5_WIDTH // LANES


def _s5_permute_in(u_ref, up_ref, tt):
    if tt == 1:
        return u_ref[:, 0, :]
    for j in range(SUBLANES):
        uj = u_ref[j]
        for k in range(S5_TILES):
            up_ref[k, pl.ds(j, tt, stride=SUBLANES), :] = uj[:, k * LANES:(k + 1) * LANES]
    return jnp.concatenate([up_ref[k] for k in range(S5_TILES)], axis=1)


def _s5_permute_out(res, y_ref, o_ref, tt):
    if tt == 1:
        o_ref[:, 0, :] = res
        return
    for k in range(S5_TILES):
        y_ref[k] = res[:, k * LANES:(k + 1) * LANES]
    for j in range(SUBLANES):
        o_ref[j] = jnp.concatenate(
            [y_ref[k, pl.ds(j, tt, stride=SUBLANES), :] for k in range(S5_TILES)], axis=1)


def _s5_scan(bu_ref, ar_ref, ai_ref, hr_ref, hi_ref, tt, store):
    for lb in range(S5_LANES // S5_LANE_BLOCK):
        re = slice(lb * S5_LANE_BLOCK, (lb + 1) * S5_LANE_BLOCK)
        im = slice(S5_LANES + lb * S5_LANE_BLOCK, S5_LANES + (lb + 1) * S5_LANE_BLOCK)
        ar = jnp.broadcast_to(ar_ref[:, re], (SUBLANES, S5_LANE_BLOCK))
        ai = jnp.broadcast_to(ai_ref[:, re], (SUBLANES, S5_LANE_BLOCK))

        def step(t, carry):
            hr, hi = carry
            r0 = pl.multiple_of(t * SUBLANES, SUBLANES)
            nhr = ar * hr - ai * hi + bu_ref[pl.ds(r0, SUBLANES), re]
            nhi = ar * hi + ai * hr + bu_ref[pl.ds(r0, SUBLANES), im]
            if store:
                bu_ref[pl.ds(r0, SUBLANES), re] = nhr
                bu_ref[pl.ds(r0, SUBLANES), im] = nhi
            return nhr, nhi

        hr, hi = lax.fori_loop(0, tt, step, (hr_ref[:, re], hi_ref[:, re]), unroll=min(tt, 8))
        hr_ref[:, re] = hr
        hi_ref[:, re] = hi


def _s5_ends_kernel(u_ref, bmat_ref, ar_ref, ai_ref, e_ref, up_ref, bu_ref, hr_ref, hi_ref, *, tt):
    i = pl.program_id(1)

    @pl.when(i == 0)
    def _():
        hr_ref[...] = jnp.zeros(hr_ref.shape, F32)
        hi_ref[...] = jnp.zeros(hi_ref.shape, F32)

    up = _s5_permute_in(u_ref, up_ref, tt)
    bu_ref[...] = _dot(up.astype(BF16), bmat_ref[...])
    _s5_scan(bu_ref, ar_ref, ai_ref, hr_ref, hi_ref, tt, store=False)

    @pl.when(i == pl.num_programs(1) - 1)
    def _():
        e_ref[:, 0:S5_LANES] = hr_ref[...]
        e_ref[:, S5_LANES:] = hi_ref[...]


def _s5_carry_kernel(e_ref, pr_ref, pi_ref, h0_ref):
    e = e_ref[...]
    pr = pr_ref[...]
    pi = pi_ref[...]
    cr = jnp.zeros((1, S5_LANES), F32)
    ci = jnp.zeros((1, S5_LANES), F32)
    for j in range(SUBLANES):
        h0_ref[j:j + 1, 0:S5_LANES] = cr
        h0_ref[j:j + 1, S5_LANES:] = ci
        er = e[j:j + 1, 0:S5_LANES]
        ei = e[j:j + 1, S5_LANES:]
        cr, ci = pr * cr - pi * ci + er, pr * ci + pi * cr + ei


def _s5_main_kernel(u_ref, h0_ref, bmat_ref, cmat_ref, ar_ref, ai_ref, d_ref, wglu_ref, o_ref, hl_ref,
                    up_ref, bu_ref, y_ref, hr_ref, hi_ref, *, tt):
    i = pl.program_id(1)

    @pl.when(i == 0)
    def _():
        hr_ref[...] = h0_ref[:, 0:S5_LANES]
        hi_ref[...] = h0_ref[:, S5_LANES:]

    up = _s5_permute_in(u_ref, up_ref, tt)
    bu_ref[...] = _dot(up.astype(BF16), bmat_ref[...])
    _s5_scan(bu_ref, ar_ref, ai_ref, hr_ref, hi_ref, tt, store=True)

    @pl.when(i == pl.num_programs(1) - 1)
    def _():
        hl_ref[:, 0:S5_LANES] = hr_ref[...]
        hl_ref[:, S5_LANES:] = hi_ref[...]

    y = _dot(bu_ref[...].astype(BF16), cmat_ref[...]) + d_ref[...] * up
    g = jax.nn.gelu(y)
    res = g * jax.nn.sigmoid(_dot(g.astype(BF16), wglu_ref[...]))
    _s5_permute_out(res, y_ref, o_ref, tt)


def _s5_specs(nb, lseg, tt, col):
    u_spec = pl.BlockSpec((None, SUBLANES, tt, S5_WIDTH), lambda b, i: (b, 0, i, col))
    st_spec = pl.BlockSpec((None, SUBLANES, 2 * S5_LANES), lambda b, i: (b, 0, 0))

    def full(shape):
        return pl.BlockSpec(shape, lambda b, i: (0,) * len(shape))

    return u_spec, st_spec, full


def _s5_ends(u4, col, bmat, ar, ai, tt):
    nb, _, lseg, _ = u4.shape
    u_spec, st_spec, full = _s5_specs(nb, lseg, tt, col)
    return pl.pallas_call(
        functools.partial(_s5_ends_kernel, tt=tt),
        grid=(nb, lseg // tt),
        in_specs=[u_spec, full((S5_WIDTH, 2 * S5_LANES)), full((1, S5_LANES)), full((1, S5_LANES))],
        out_specs=st_spec,
        out_shape=jax.ShapeDtypeStruct((nb, SUBLANES, 2 * S5_LANES), F32),
        scratch_shapes=[
            pltpu.VMEM((S5_TILES, tt * SUBLANES, LANES), F32),
            pltpu.VMEM((tt * SUBLANES, 2 * S5_LANES), F32),
            pltpu.VMEM((SUBLANES, S5_LANES), F32),
            pltpu.VMEM((SUBLANES, S5_LANES), F32),
        ],
        compiler_params=_cparams("parallel", "arbitrary"),
        name="s5_ends",
    )(u4, bmat, ar, ai)


def _s5_carry(e, pr, pi):
    nb = e.shape[0]
    st_spec = pl.BlockSpec((None, SUBLANES, 2 * S5_LANES), lambda b: (b, 0, 0))
    p_spec = pl.BlockSpec((1, S5_LANES), lambda b: (0, 0))
    return pl.pallas_call(
        _s5_carry_kernel,
        grid=(nb,),
        in_specs=[st_spec, p_spec, p_spec],
        out_specs=st_spec,
        out_shape=jax.ShapeDtypeStruct(e.shape, F32),
        compiler_params=_cparams("parallel"),
        name="s5_carry",
    )(e, pr, pi)


def _s5_main(u4, col, h0, bmat, cmat, ar, ai, dskip, wglu, tt):
    nb, _, lseg, _ = u4.shape
    u_spec, st_spec, full = _s5_specs(nb, lseg, tt, col)
    o_spec = pl.BlockSpec((None, SUBLANES, tt, S5_WIDTH), lambda b, i: (b, 0, i, 0))
    return pl.pallas_call(
        functools.partial(_s5_main_kernel, tt=tt),
        grid=(nb, lseg // tt),
        in_specs=[u_spec, st_spec, full((S5_WIDTH, 2 * S5_LANES)), full((2 * S5_LANES, S5_WIDTH)),
                  full((1, S5_LANES)), full((1, S5_LANES)), full((1, S5_WIDTH)), full((S5_WIDTH, S5_WIDTH))],
        out_specs=[o_spec, st_spec],
        out_shape=[jax.ShapeDtypeStruct((nb, SUBLANES, lseg, S5_WIDTH), F32),
                   jax.ShapeDtypeStruct((nb, SUBLANES, 2 * S5_LANES), F32)],
        scratch_shapes=[
            pltpu.VMEM((S5_TILES, tt * SUBLANES, LANES), F32),
            pltpu.VMEM((tt * SUBLANES, 2 * S5_LANES), F32),
            pltpu.VMEM((S5_TILES, tt * SUBLANES, LANES), F32),
            pltpu.VMEM((SUBLANES, S5_LANES), F32),
            pltpu.VMEM((SUBLANES, S5_LANES), F32),
        ],
        compiler_params=_cparams("parallel", "arbitrary"),
        name="s5_main",
    )(u4, h0, bmat, cmat, ar, ai, dskip, wglu)


def _softplus(x):
    return jnp.maximum(x, 0.0) + jnp.log1p(jnp.exp(-jnp.abs(x)))


def _silu(x):
    return x * jax.nn.sigmoid(x)


def _ssd_gate_norm(y, z, nrm):
    y = y * _silu(z)
    outs = []
    gw = SSM_INNER // SSM_GROUPS
    for g in range(SSM_GROUPS):
        yg = y[:, g * gw:(g + 1) * gw]
        ms = jnp.mean(yg * yg, axis=-1, keepdims=True)
        outs.append(yg * lax.rsqrt(ms + EPS))
    return jnp.concatenate(outs, axis=1) * nrm


def _ssd_kernel(xbc_ref, z_ref, dt_ref, cw_ref, cb_ref, dtb_ref, alog_ref, dexp_ref, nrm_ref, exp_ref,
                o_ref, st_ref, xpad_ref, stt_ref):
    c = pl.program_id(1)
    t = SSD_CHUNK
    gw = SSM_INNER // SSM_GROUPS

    @pl.when(c == 0)
    def _():
        xpad_ref[0:SUBLANES, :] = jnp.zeros((SUBLANES, CONV_DIM), F32)
        stt_ref[...] = jnp.zeros(stt_ref.shape, F32)

    x = xbc_ref[...]
    xpad_ref[SUBLANES:SUBLANES + t, :] = x
    conv = cb_ref[...]
    for k in range(CONV_WIDTH):
        conv = conv + xpad_ref[pl.ds(SUBLANES - (CONV_WIDTH - 1) + k, t), :] * cw_ref[k:k + 1, :]
    xpad_ref[0:SUBLANES, :] = x[t - SUBLANES:t, :]
    act = _silu(conv)
    xs = act[:, 0:SSM_INNER]
    bm = act[:, SSM_INNER:SSM_INNER + SSM_GROUPS * SSM_STATE]
    cm = act[:, SSM_INNER + SSM_GROUPS * SSM_STATE:]

    dt = _softplus(dt_ref[...] + dtb_ref[...])
    a = -jnp.exp(alog_ref[...])
    row = lax.broadcasted_iota(I32, (t, t), 0)
    colm = lax.broadcasted_iota(I32, (t, t), 1)
    causal = row >= colm
    ltri = jnp.where(causal, 1.0, 0.0).astype(BF16)
    cs = _dot3_exact_lhs(ltri, dt * a)
    cs_t = cs.T
    dt_t = dt.T
    ecs = jnp.exp(cs)
    wgt = jnp.exp(cs[t - 1:t, :] - cs) * dt
    expand = exp_ref[...]
    ecs_x = _dot3_exact_rhs(ecs, expand)
    wgt_x = _dot3_exact_rhs(wgt, expand)

    lane = lax.broadcasted_iota(I32, (t, LANES), 1)
    lo_half = lane < SSM_HEAD_DIM
    xs_b = xs.astype(BF16)
    zero_b = jnp.zeros((t, LANES), BF16)
    y_tiles = []
    for hp in range(SSM_HEADS // 2):
        g = hp // (SSM_HEADS // 2 // SSM_GROUPS)
        cb = _dot_nt(cm[:, g * SSM_STATE:(g + 1) * SSM_STATE].astype(BF16),
                     bm[:, g * SSM_STATE:(g + 1) * SSM_STATE].astype(BF16))
        xt = xs_b[:, hp * LANES:(hp + 1) * LANES]
        acc = None
        for hh in range(2):
            h = 2 * hp + hh
            seg = cs[:, h:h + 1] - cs_t[h:h + 1, :]
            dec = jnp.exp(jnp.where(causal, seg, -jnp.inf))
            w = (cb * dec * dt_t[h:h + 1, :]).astype(BF16)
            xm = jnp.where(lo_half if hh == 0 else ~lo_half, xt, zero_b)
            part = _dot(w, xm)
            acc = part if acc is None else acc + part
        y_tiles.append(acc)
    y = jnp.concatenate(y_tiles, axis=1)

    xw = (xs * wgt_x).astype(BF16)
    cdec = ecs_x[t - 1:t, :]
    y_off = []
    for g in range(SSM_GROUPS):
        st = stt_ref[g]
        cg = cm[:, g * SSM_STATE:(g + 1) * SSM_STATE].astype(BF16)
        y_off.append(_dot(cg, st.astype(BF16)) * ecs_x[:, g * gw:(g + 1) * gw])
        bg = bm[:, g * SSM_STATE:(g + 1) * SSM_STATE].astype(BF16)
        stt_ref[g] = st * cdec[:, g * gw:(g + 1) * gw] + _dot_tn(bg, xw[:, g * gw:(g + 1) * gw])
    y = y + jnp.concatenate(y_off, axis=1) + dexp_ref[...] * xs
    o_ref[...] = _ssd_gate_norm(y, z_ref[...], nrm_ref[...])

    @pl.when(c == pl.num_programs(1) - 1)
    def _():
        for g in range(SSM_GROUPS):
            st_ref[g * gw:(g + 1) * gw, :] = stt_ref[g].T


def _ssd_prompt(h, cw, cb, dtb, alog, dexp, nrm, expand, nb, lt):
    nc = lt // SSD_CHUNK

    def col(width, off):
        return pl.BlockSpec((SSD_CHUNK, width), lambda b, c: (b * nc + c, off // width))

    def full(shape):
        return pl.BlockSpec(shape, lambda b, c: (0, 0))

    return pl.pallas_call(
        _ssd_kernel,
        grid=(nb, nc),
        in_specs=[col(CONV_DIM, OFF_XBC), col(SSM_INNER, OFF_Z), col(LANES, OFF_DT),
                  full((CONV_WIDTH, CONV_DIM)), full((1, CONV_DIM)), full((1, LANES)), full((1, LANES)),
                  full((1, SSM_INNER)), full((1, SSM_INNER)), full((LANES, SSM_INNER))],
        out_specs=[pl.BlockSpec((SSD_CHUNK, SSM_INNER), lambda b, c: (b * nc + c, 0)),
                   pl.BlockSpec((None, SSM_INNER, SSM_STATE), lambda b, c: (b, 0, 0))],
        out_shape=[jax.ShapeDtypeStruct((nb * lt, SSM_INNER), F32),
                   jax.ShapeDtypeStruct((nb, SSM_INNER, SSM_STATE), F32)],
        scratch_shapes=[pltpu.VMEM((SUBLANES + SSD_CHUNK, CONV_DIM), F32),
                        pltpu.VMEM((SSM_GROUPS, SSM_STATE, SSM_INNER // SSM_GROUPS), F32)],
        compiler_params=_cparams("parallel", "arbitrary"),
        name="ssd_prompt",
    )(h, h, h, cw, cb, dtb, alog, dexp, nrm, expand)


def _merge_kernel(x_ref, g0_ref, g1_ref, g2_ref, os5_ref, oatt_ref, ossm_ref, w1_ref, w2_ref, w3_ref, wo_ref,
                  o_ref):
    merged = (jax.nn.sigmoid(g0_ref[...]) * _dot(os5_ref[...].astype(BF16), w1_ref[...])
              + jax.nn.sigmoid(g1_ref[...]) * _dot(oatt_ref[...].astype(BF16), w2_ref[...])
              + jax.nn.sigmoid(g2_ref[...]) * _dot(ossm_ref[...].astype(BF16), w3_ref[...]))
    o_ref[...] = x_ref[...] + _dot(merged.astype(BF16), wo_ref[...])


def _merge(x, h, os5, oatt, ossm, w1, w2, w3, wo, tm):
    m = x.shape[0]

    def rows(width, blk=0):
        return pl.BlockSpec((tm, width), lambda i: (i, blk))

    def full(shape):
        return pl.BlockSpec(shape, lambda i: (0, 0))

    return pl.pallas_call(
        _merge_kernel,
        grid=(m // tm,),
        in_specs=[rows(D_MODEL), rows(D_MODEL, 0), rows(D_MODEL, 1), rows(D_MODEL, 2),
                  rows(S5_WIDTH), rows(ATT_WIDTH), rows(SSM_INNER),
                  full((S5_WIDTH, D_MODEL)), full((ATT_WIDTH, D_MODEL)), full((SSM_INNER, D_MODEL)),
                  full((D_MODEL, D_MODEL))],
        out_specs=rows(D_MODEL),
        out_shape=jax.ShapeDtypeStruct((m, D_MODEL), F32),
        compiler_params=_cparams("parallel"),
        name="merge",
    )(x, h, h, h, os5, oatt, ossm, w1, w2, w3, wo)


def _mlp_kernel(x_ref, g_ref, wu_ref, wd_ref, o_ref):
    x = x_ref[...]
    ms = jnp.mean(x * x, axis=-1, keepdims=True)
    xn = (x * lax.rsqrt(ms + EPS) * g_ref[...]).astype(BF16)
    hid = jnp.maximum(_dot(xn, wu_ref[...]), 0.0)
    o_ref[...] = x + _dot((hid * hid).astype(BF16), wd_ref[...])


def _mlp(x, g, wu, wd, tm):
    m = x.shape[0]
    return pl.pallas_call(
        _mlp_kernel,
        grid=(m // tm,),
        in_specs=[pl.BlockSpec((tm, D_MODEL), lambda i: (i, 0)),
                  pl.BlockSpec((1, D_MODEL), lambda i: (0, 0)),
                  pl.BlockSpec((D_MODEL, D_FF), lambda i: (0, 0)),
                  pl.BlockSpec((D_FF, D_MODEL), lambda i: (0, 0))],
        out_specs=pl.BlockSpec((tm, D_MODEL), lambda i: (i, 0)),
        out_shape=jax.ShapeDtypeStruct((m, D_MODEL), F32),
        compiler_params=_cparams("parallel"),
        name="mlp",
    )(x, g, wu, wd)


def _ssd_step_kernel(xbc_ref, z_ref, dt_ref, cst_ref, st_ref, cw_ref, cb_ref, dtb_ref, alog_ref, dexp_ref,
                     nrm_ref, exp_ref, o_ref, cst_out_ref, st_out_ref):
    ns = SUBLANES
    gw = SSM_INNER // SSM_GROUPS
    x = xbc_ref[...]
    conv = cb_ref[...]
    for k in range(CONV_WIDTH - 1):
        conv = conv + cst_ref[k] * cw_ref[k:k + 1, :]
    conv = conv + x * cw_ref[CONV_WIDTH - 1:CONV_WIDTH, :]
    for k in range(CONV_WIDTH - 2):
        cst_out_ref[k] = cst_ref[k + 1]
    cst_out_ref[CONV_WIDTH - 2] = x
    act = _silu(conv)
    xs = act[:, 0:SSM_INNER]
    bm = act[:, SSM_INNER:SSM_INNER + SSM_GROUPS * SSM_STATE]
    cm = act[:, SSM_INNER + SSM_GROUPS * SSM_STATE:]
    dt = _softplus(dt_ref[...] + dtb_ref[...])
    decay = jnp.exp(dt * (-jnp.exp(alog_ref[...])))
    expand = exp_ref[...]
    decay_x = _dot3_exact_rhs(decay, expand)
    xdt = xs * _dot3_exact_rhs(dt, expand)
    rowid = lax.broadcasted_iota(I32, (ns, 1), 0)
    ones = jnp.ones((ns, SSM_STATE), BF16)
    y = jnp.zeros((ns, SSM_INNER), F32)
    y_parts = [[], []]
    for s in range(ns):
        pick = rowid == s
        for g in range(SSM_GROUPS):
            cols = slice(g * gw, (g + 1) * gw)
            nsl = slice(g * SSM_STATE, (g + 1) * SSM_STATE)
            dsel = jnp.where(pick, decay_x[:, cols], 0.0)
            dh, dm, dl = _split3(dsel)
            dmat = _dot_tn(dh, ones) + _dot_tn(dm, ones) + _dot_tn(dl, ones)
            xsel = jnp.where(pick, xdt[:, cols], 0.0)
            xh = xsel.astype(BF16)
            xl = (xsel - xh.astype(F32)).astype(BF16)
            bg = bm[:, nsl].astype(BF16)
            new = dmat * st_ref[s, cols, :] + _dot_tn(xh, bg) + _dot_tn(xl, bg)
            st_out_ref[s, cols, :] = new
            csel = jnp.where(pick, cm[:, nsl], 0.0).astype(BF16)
            y_parts[g].append(_dot_nt(csel, new.astype(BF16)))
    halves = []
    for g in range(SSM_GROUPS):
        acc = y_parts[g][0]
        for part in y_parts[g][1:]:
            acc = acc + part
        halves.append(acc)
    y = jnp.concatenate(halves, axis=1) + dexp_ref[...] * xs
    o_ref[...] = _ssd_gate_norm(y, z_ref[...], nrm_ref[...])


def _ssd_step(h, cst, st, cw, cb, dtb, alog, dexp, nrm, expand):
    ns = h.shape[0]
    g8 = SUBLANES

    def col(width, off):
        return pl.BlockSpec((g8, width), lambda i: (i, off // width))

    def full(shape):
        return pl.BlockSpec(shape, lambda i: (0, 0))

    cst_spec = pl.BlockSpec((CONV_WIDTH - 1, g8, CONV_DIM), lambda i: (0, i, 0))
    st_spec = pl.BlockSpec((g8, SSM_INNER, SSM_STATE), lambda i: (i, 0, 0))
    return pl.pallas_call(
        _ssd_step_kernel,
        grid=(ns // g8,),
        in_specs=[col(CONV_DIM, OFF_XBC), col(SSM_INNER, OFF_Z), col(LANES, OFF_DT), cst_spec, st_spec,
                  full((CONV_WIDTH, CONV_DIM)), full((1, CONV_DIM)), full((1, LANES)), full((1, LANES)),
                  full((1, SSM_INNER)), full((1, SSM_INNER)), full((LANES, SSM_INNER))],
        out_specs=[pl.BlockSpec((g8, SSM_INNER), lambda i: (i, 0)), cst_spec, st_spec],
        out_shape=[jax.ShapeDtypeStruct((ns, SSM_INNER), F32),
                   jax.ShapeDtypeStruct(cst.shape, F32),
                   jax.ShapeDtypeStruct(st.shape, F32)],
        compiler_params=_cparams("parallel"),
        name="ssd_step",
    )(h, h, h, cst, st, cw, cb, dtb, alog, dexp, nrm, expand)


PAGES_PER_CHUNK = 16
SAMPLE_KC = PAGES_PER_CHUNK * PAGE_SIZE


def _dsa_sample_kernel(pt_ref, qb_ref, iqh_ref, iql_ref, ikw_ref, kn_ref, vn_ref, ikn_ref,
                       ck_hbm, cv_hbm, cik_hbm, o_ref,
                       ikbuf, kbuf, vbuf, key_ref, sem_ik, sem_kv, *, n_pages, page_base, k_sel, idx_bits):
    b = pl.program_id(0)
    kc = SAMPLE_KC
    n_chunks = n_pages // PAGES_PER_CHUNK
    past = n_pages * PAGE_SIZE

    def ik_copy(p):
        return pltpu.make_async_copy(cik_hbm.at[page_base + pt_ref[b, p]], ikbuf.at[p], sem_ik)

    def kv_copies(c, i):
        page = page_base + pt_ref[b, c * PAGES_PER_CHUNK + i]
        slot = c % 2
        return (pltpu.make_async_copy(ck_hbm.at[page], kbuf.at[slot, i], sem_kv.at[0, slot]),
                pltpu.make_async_copy(cv_hbm.at[page], vbuf.at[slot, i], sem_kv.at[1, slot]))

    def start_kv(c):
        def body(i, _):
            ck, cv = kv_copies(c, i)
            ck.start()
            cv.start()
            return 0
        lax.fori_loop(0, PAGES_PER_CHUNK, body, 0)

    def wait_kv(c):
        def body(i, _):
            ck, cv = kv_copies(c, i)
            ck.wait()
            cv.wait()
            return 0
        lax.fori_loop(0, PAGES_PER_CHUNK, body, 0)

    def start_ik(p, _):
        ik_copy(p).start()
        return 0

    def wait_ik(p, _):
        ik_copy(p).wait()
        return 0

    lax.fori_loop(0, n_pages, start_ik, 0)
    start_kv(0)
    lax.fori_loop(0, n_pages, wait_ik, 0)

    iqh = iqh_ref[...]
    iql = iql_ref[...]
    zrows = jnp.zeros((SUBLANES - IDX_HEADS, IDX_DIM), BF16)
    lh = jnp.concatenate([iqh[:, h * IDX_DIM:(h + 1) * IDX_DIM] for h in range(IDX_HEADS)] + [zrows], axis=0)
    ll = jnp.concatenate([iql[:, h * IDX_DIM:(h + 1) * IDX_DIM] for h in range(IDX_HEADS)] + [zrows], axis=0)
    iw = ikw_ref[...]
    iw_col = jnp.concatenate([iw[:, IDX_DIM + h:IDX_DIM + h + 1] for h in range(IDX_HEADS)]
                             + [jnp.zeros((SUBLANES - IDX_HEADS, 1), F32)], axis=0)

    def score_of(kh, kl):
        d = jnp.maximum(_dot_nt(lh, kh) + _dot_nt(lh, kl) + _dot_nt(ll, kh), 0.0)
        return jnp.sum(iw_col * d, axis=0, keepdims=True) * IDX_SCALE

    def idx_body(c, _):
        p0 = pl.multiple_of(c * PAGES_PER_CHUNK, PAGES_PER_CHUNK)
        ik = ikbuf[pl.ds(p0, PAGES_PER_CHUNK)].reshape(kc, IDX_DIM)
        kh = ik.astype(BF16)
        kl = (ik - kh.astype(F32)).astype(BF16)
        key_ref[c] = _sort_key(score_of(kh, kl))
        return 0

    lax.fori_loop(0, n_chunks, idx_body, 0)
    ikn = ikn_ref[...]
    ikn8 = jnp.concatenate([ikn, jnp.zeros((SUBLANES - 1, IDX_DIM), F32)], axis=0)
    nh = ikn8.astype(BF16)
    nl = (ikn8 - nh.astype(F32)).astype(BF16)
    s_new = score_of(nh, nl)[:, 0:1]
    lane = lax.broadcasted_iota(I32, (1, kc), 1)
    key_ref[n_chunks] = _sort_key(jnp.where(lane == 0, s_new, -jnp.inf))

    t, jcut = _topk_threshold(key_ref, n_chunks + 1, kc, 1, k_sel, idx_bits)

    def selected(c):
        kk = key_ref[c]
        pos = c * kc + lane
        return ((kk > t) | ((kk == t) & (pos <= jcut))) & (pos <= past)

    q = qb_ref[...].astype(F32)
    rows = []
    for h in range(N_HEADS):
        g = h // 2
        tile = q[:, g * LANES:(g + 1) * LANES]
        src = tile if (g % 2) == (h % 2) else pltpu.roll(tile, HEAD_DIM, 1)
        lo = (g % 2) * HEAD_DIM
        lane_t = lax.broadcasted_iota(I32, (1, LANES), 1)
        placed = jnp.where((lane_t >= lo) & (lane_t < lo + HEAD_DIM), src, 0.0)
        zero = jnp.zeros((1, LANES), F32)
        rows.append(jnp.concatenate([placed, zero] if g < 2 else [zero, placed], axis=1))
    qrows = jnp.concatenate(rows, axis=0)
    lo_lane = (lax.broadcasted_iota(I32, (N_HEADS, KV_WIDTH), 0) // 2) * HEAD_DIM
    lane8 = lax.broadcasted_iota(I32, (N_HEADS, KV_WIDTH), 1)
    keep = (lane8 >= lo_lane) & (lane8 < lo_lane + HEAD_DIM)
    qrows_b = qrows.astype(BF16)

    def att_body(c, carry):
        m_old, l_old, acc = carry

        @pl.when(c + 1 < n_chunks)
        def _():
            start_kv(c + 1)

        wait_kv(c)
        slot = c % 2
        kblk = kbuf[slot].reshape(kc, KV_WIDTH).astype(BF16)
        vblk = vbuf[slot].reshape(kc, KV_WIDTH).astype(BF16)
        s = jnp.where(selected(c), _dot_nt(qrows_b, kblk), NEG_BIG)
        m_new = jnp.maximum(m_old, jnp.max(s, axis=1, keepdims=True))
        alpha = jnp.exp2(m_old - m_new)
        p = jnp.exp2(s - m_new)
        l_new = alpha * l_old + jnp.sum(p, axis=1, keepdims=True)
        acc = alpha * acc + _dot(p.astype(BF16), vblk)
        return m_new, l_new, acc

    init = (jnp.full((N_HEADS, 1), NEG_BIG, F32), jnp.zeros((N_HEADS, 1), F32),
            jnp.zeros((N_HEADS, KV_WIDTH), F32))
    m_old, l_old, acc = lax.fori_loop(0, n_chunks, att_body, init)

    kn = kn_ref[...].astype(F32)
    vn = vn_ref[...].astype(F32)
    sel_new = selected(n_chunks)[:, 0:1]
    s = jnp.where(sel_new, jnp.sum(qrows_b.astype(F32) * kn, axis=1, keepdims=True), NEG_BIG)
    m_new = jnp.maximum(m_old, s)
    alpha = jnp.exp2(m_old - m_new)
    p = jnp.exp2(s - m_new)
    l_new = alpha * l_old + p
    acc = alpha * acc + p.astype(BF16).astype(F32) * vn
    res = jnp.where(keep, acc / l_new, 0.0)
    out_r = []
    for r in range(2):
        row = res[r:r + 1, :]
        for g in range(1, N_KV_HEADS):
            row = row + res[2 * g + r:2 * g + r + 1, :]
        out_r.append(row)
    o_ref[...] = jnp.concatenate(out_r, axis=1)


def _dsa_sample(page_table, qb, iqh, iql, h, kb, vb, iko, cache_k, cache_v, cache_ik, layer):
    ns, n_pages = page_table.shape
    depth, n_pool = cache_k.shape[0], cache_k.shape[1]
    total = n_pages * PAGE_SIZE + 1
    k_sel = min(TOPK_MAX, total // 4)
    idx_bits = max(1, (total - 1).bit_length())
    n_chunks = n_pages // PAGES_PER_CHUNK

    def row(width):
        return pl.BlockSpec((None, 1, width), lambda b, pt: (b, 0, 0))

    any_spec = pl.BlockSpec(memory_space=pl.ANY)
    ikw = h[:, OFF_IKW:OFF_IKW + LANES]
    grid_spec = pltpu.PrefetchScalarGridSpec(
        num_scalar_prefetch=1,
        grid=(ns,),
        in_specs=[row(ATT_WIDTH), row(KV_WIDTH), row(KV_WIDTH), row(LANES), row(KV_WIDTH), row(KV_WIDTH),
                  row(IDX_DIM), any_spec, any_spec, any_spec],
        out_specs=row(ATT_WIDTH),
        scratch_shapes=[
            pltpu.VMEM((n_pages, PAGE_SIZE, IDX_DIM), F32),
            pltpu.VMEM((2, PAGES_PER_CHUNK, PAGE_SIZE, KV_WIDTH), F32),
            pltpu.VMEM((2, PAGES_PER_CHUNK, PAGE_SIZE, KV_WIDTH), F32),
            pltpu.VMEM((n_chunks + 1, 1, SAMPLE_KC), I32),
            pltpu.SemaphoreType.DMA(()),
            pltpu.SemaphoreType.DMA((2, 2)),
        ],
    )
    r3 = lambda a: a.reshape(ns, 1, a.shape[-1])
    out = pl.pallas_call(
        functools.partial(_dsa_sample_kernel, n_pages=n_pages, page_base=layer * n_pool, k_sel=k_sel,
                          idx_bits=idx_bits),
        grid_spec=grid_spec,
        out_shape=jax.ShapeDtypeStruct((ns, 1, ATT_WIDTH), F32),
        compiler_params=_cparams("arbitrary"),
        name="dsa_sample",
    )(page_table, r3(qb), r3(iqh), r3(iql), r3(ikw), r3(kb), r3(vb), r3(iko),
      cache_k.reshape(depth * n_pool, PAGE_SIZE, KV_WIDTH), cache_v.reshape(depth * n_pool, PAGE_SIZE, KV_WIDTH),
      cache_ik.reshape(depth * n_pool, PAGE_SIZE, IDX_DIM))
    return out.reshape(ns, ATT_WIDTH)


def _complex_pow(re, im, n):
    rr, ri = jnp.ones_like(re), jnp.zeros_like(im)
    while n:
        if n & 1:
            rr, ri = rr * re - ri * im, rr * im + ri * re
        re, im = re * re - im * im, 2.0 * re * im
        n >>= 1
    return rr, ri


def _layer_params(lw, lseg):
    w = lw["w_in"]
    offs = [0]
    for s in IN_SIZES:
        offs.append(offs[-1] + s)
    u_, q_, k_, v_, iq_, ik_, iw_, z_, xbc_, dt_, gt_ = [w[:, offs[i]:offs[i + 1]] for i in range(len(IN_SIZES))]
    zpad = lambda n: jnp.zeros((D_MODEL, n), w.dtype)
    wp = jnp.concatenate([gt_, z_, u_, xbc_, q_, k_, v_, iq_, ik_, iw_, zpad(LANES - IDX_DIM - IDX_HEADS),
                          dt_, zpad(LANES - SSM_HEADS)], axis=1).astype(BF16)
    p = {"wp": wp, "norm_mix": lw["norm_mix"].reshape(1, D_MODEL)}
    p["qg"] = jnp.tile(lw["q_norm"], N_HEADS).reshape(1, ATT_WIDTH)
    p["kg"] = jnp.tile(lw["k_norm"], N_KV_HEADS).reshape(1, KV_WIDTH)
    ones = jnp.ones((HEAD_DIM, HEAD_DIM), BF16)
    p["bdq"] = jnp.kron(jnp.eye(N_HEADS, dtype=BF16), ones)
    p["bdk"] = jnp.kron(jnp.eye(N_KV_HEADS, dtype=BF16), ones)

    ar, ai = lw["s5_a_re"], lw["s5_a_im"]
    dt = jnp.exp(lw["s5_log_dt"])[:, None]
    mag = jnp.exp(dt * ar)
    abar_re, abar_im = mag * jnp.cos(dt * ai), mag * jnp.sin(dt * ai)
    den = ar * ar + ai * ai
    nr, ni = abar_re - 1.0, abar_im
    coef_re = (nr * ar + ni * ai) / den
    coef_im = (ni * ar - nr * ai) / den
    bc_re = coef_re[:, :, None] * lw["s5_b_re"] - coef_im[:, :, None] * lw["s5_b_im"]
    bc_im = coef_re[:, :, None] * lw["s5_b_im"] + coef_im[:, :, None] * lw["s5_b_re"]
    eye = jnp.eye(S5_GROUPS, dtype=F32)
    embed_b = lambda b: jnp.einsum("gnc,gh->gchn", b, eye).reshape(S5_WIDTH, S5_LANES)
    embed_c = lambda c: jnp.einsum("gcn,gh->gnhc", c, eye).reshape(S5_LANES, S5_WIDTH)
    p["bmat"] = jnp.concatenate([embed_b(bc_re), embed_b(bc_im)], axis=1).astype(BF16)
    p["cmat"] = jnp.concatenate([embed_c(lw["s5_c_re"]), -embed_c(lw["s5_c_im"])], axis=0).astype(BF16)
    p["ar"] = abar_re.reshape(1, S5_LANES)
    p["ai"] = abar_im.reshape(1, S5_LANES)
    pr, pi = _complex_pow(abar_re, abar_im, lseg)
    p["pr"] = pr.reshape(1, S5_LANES)
    p["pi"] = pi.reshape(1, S5_LANES)
    p["dskip"] = lw["s5_d"].reshape(1, S5_WIDTH)
    p["wglu"] = lw["s5_w_glu"].astype(BF16)

    p["cw"] = lw["conv_w"]
    p["cb"] = lw["conv_b"].reshape(1, CONV_DIM)
    padh = lambda v: jnp.pad(v, (0, LANES - SSM_HEADS)).reshape(1, LANES)
    p["dtb"] = padh(lw["dt_bias"])
    p["alog"] = padh(lw["a_log"])
    p["dexp"] = jnp.repeat(lw["ssm_d"], SSM_HEAD_DIM).reshape(1, SSM_INNER)
    p["nrm"] = lw["ssm_norm"].reshape(1, SSM_INNER)
    p["expand"] = jnp.pad(jnp.kron(jnp.eye(SSM_HEADS, dtype=F32), jnp.ones((1, SSM_HEAD_DIM), F32)),
                          ((0, LANES - SSM_HEADS), (0, 0))).astype(BF16)

    p["w1"] = lw["w_br_s5"].astype(BF16)
    p["w2"] = (lw["w_br_att"].reshape(N_KV_HEADS, 2, HEAD_DIM, D_MODEL).transpose(1, 0, 2, 3)
               .reshape(ATT_WIDTH, D_MODEL).astype(BF16))
    p["w3"] = lw["w_br_ssm"].astype(BF16)
    p["wo"] = lw["w_o"].astype(BF16)
    p["norm_mlp"] = lw["norm_mlp"].reshape(1, D_MODEL)
    p["wu"] = lw["w_up"].astype(BF16)
    p["wd"] = lw["w_down"].astype(BF16)
    return p


def _rope_table(pos):
    half = ROT_DIM // 2
    inv_freq = ROPE_THETA ** (-jnp.arange(half, dtype=F32) / half)
    ang = pos.astype(F32)[:, None] * inv_freq[None, :]
    cos, sin = jnp.cos(ang), jnp.sin(ang)
    n = pos.shape[0]
    rest = HEAD_DIM - ROT_DIM
    c = jnp.concatenate([cos, cos, jnp.ones((n, rest), F32)], axis=1)
    sa = jnp.concatenate([-sin, jnp.zeros((n, half + rest), F32)], axis=1)
    sb = jnp.concatenate([jnp.zeros((n, half), F32), sin, jnp.zeros((n, rest), F32)], axis=1)
    rep = LANES // HEAD_DIM
    return jnp.concatenate([jnp.tile(c, (1, rep)), jnp.tile(sa, (1, rep)), jnp.tile(sb, (1, rep))], axis=1)


def _pick(n, pref):
    for t in pref:
        if n % t == 0:
            return t
    return n


def _layer_prompt(x2, p, tab, nb, lt):
    m = nb * lt
    lseg = lt // SUBLANES
    h = _inproj(x2, p["norm_mix"], p["wp"], _pick(m, (1024, 512, 256, 128)))
    kc = _pick(lt, (512, 256, 128))
    qb, ko, kb, vo, vt, iqh, iql, iko, ikc = _post(
        h, tab, p["qg"], p["kg"], p["bdq"], p["bdk"], nb, lt, kc, True)
    oatt = _dsa_prompt(qb, iqh, iql, h, kb, vt, ikc, nb, lt, kc)

    u4 = h.reshape(nb, SUBLANES, lseg, NP)
    tt = _pick(lseg, (64, 32, 16, 8))
    ends = _s5_ends(u4, OFF_U // S5_WIDTH, p["bmat"], p["ar"], p["ai"], tt)
    h0 = _s5_carry(ends, p["pr"], p["pi"])
    os5, hl = _s5_main(u4, OFF_U // S5_WIDTH, h0, p["bmat"], p["cmat"], p["ar"], p["ai"], p["dskip"],
                       p["wglu"], tt)

    ossm, st = _ssd_prompt(h, p["cw"], p["cb"], p["dtb"], p["alog"], p["dexp"], p["nrm"], p["expand"], nb, lt)

    tm = _pick(m, (512, 256, 128))
    x1 = _merge(x2, h, os5.reshape(m, S5_WIDTH), oatt, ossm, p["w1"], p["w2"], p["w3"], p["wo"], tm)
    xo = _mlp(x1, p["norm_mlp"], p["wu"], p["wd"], _pick(m, (256, 128)))

    new_conv = h.reshape(nb, lt, NP)[:, lt - (CONV_WIDTH - 1):, OFF_XBC:OFF_XBC + CONV_DIM]
    state = (ko.reshape(nb, lt, N_KV_HEADS, HEAD_DIM), vo.reshape(nb, lt, N_KV_HEADS, HEAD_DIM),
             iko.reshape(nb, lt, IDX_DIM),
             hl[:, SUBLANES - 1, 0:S5_LANES].reshape(nb, S5_GROUPS, S5_STATE),
             hl[:, SUBLANES - 1, S5_LANES:].reshape(nb, S5_GROUPS, S5_STATE),
             new_conv, st.reshape(nb, SSM_HEADS, SSM_HEAD_DIM, SSM_STATE))
    return xo, state


def _layer_sample(x2, p, tab, caches, states):
    cache_k, cache_v, cache_ik, page_table, layer = caches
    s5_re, s5_im, conv0, ssm0 = states
    ns = x2.shape[0]
    ng = ns // SUBLANES
    h = _inproj(x2, p["norm_mix"], p["wp"], ns)
    qb, ko, kb, vo, vb, iqh, iql, iko, _ = _post(
        h, tab, p["qg"], p["kg"], p["bdq"], p["bdk"], 1, ns, ns, False)
    oatt = _dsa_sample(page_table, qb, iqh, iql, h, kb, vb, iko, cache_k, cache_v, cache_ik, layer)

    u4 = h.reshape(ng, SUBLANES, 1, NP)
    h0 = jnp.concatenate([s5_re.reshape(ng, SUBLANES, S5_LANES), s5_im.reshape(ng, SUBLANES, S5_LANES)], axis=2)
    os5, hl = _s5_main(u4, OFF_U // S5_WIDTH, h0, p["bmat"], p["cmat"], p["ar"], p["ai"], p["dskip"],
                       p["wglu"], 1)

    ossm, cst, st = _ssd_step(h, conv0.transpose(1, 0, 2), ssm0.reshape(ns, SSM_INNER, SSM_STATE),
                              p["cw"], p["cb"], p["dtb"], p["alog"], p["dexp"], p["nrm"], p["expand"])

    x1 = _merge(x2, h, os5.reshape(ns, S5_WIDTH), oatt, ossm, p["w1"], p["w2"], p["w3"], p["wo"], ns)
    xo = _mlp(x1, p["norm_mlp"], p["wu"], p["wd"], ns)
    state = (ko.reshape(ns, 1, N_KV_HEADS, HEAD_DIM), vo.reshape(ns, 1, N_KV_HEADS, HEAD_DIM),
             iko.reshape(ns, 1, IDX_DIM),
             hl[:, :, 0:S5_LANES].reshape(ns, S5_GROUPS, S5_STATE),
             hl[:, :, S5_LANES:].reshape(ns, S5_GROUPS, S5_STATE),
             cst.transpose(1, 0, 2), st.reshape(ns, SSM_HEADS, SSM_HEAD_DIM, SSM_STATE))
    return xo, state


_WEIGHTS =("norm_mix", "w_in", "q_norm", "k_norm", "s5_a_re", "s5_a_im", "s5_log_dt", "s5_b_re", "s5_b_im",
            "s5_c_re", "s5_c_im", "s5_d", "s5_w_glu", "conv_w", "conv_b", "dt_bias", "a_log", "ssm_d", "ssm_norm",
            "w_br_s5", "w_br_att", "w_br_ssm", "w_o", "norm_mlp", "w_up", "w_down")


def kernel(x_prompt, x_sample, cache_k, cache_v, cache_idx_k, state_s5_re, state_s5_im, state_conv, state_ssm, page_table, norm_mix, w_in, q_norm, k_norm, s5_a_re, s5_a_im, s5_log_dt, s5_b_re, s5_b_im, s5_c_re, s5_c_im, s5_d, s5_w_glu, conv_w, conv_b, dt_bias, a_log, ssm_d, ssm_norm, w_br_s5, w_br_att, w_br_ssm, w_o, norm_mlp, w_up, w_down):
    weights = dict(zip(_WEIGHTS, (norm_mix, w_in, q_norm, k_norm, s5_a_re, s5_a_im, s5_log_dt, s5_b_re, s5_b_im,
                                  s5_c_re, s5_c_im, s5_d, s5_w_glu, conv_w, conv_b, dt_bias, a_log, ssm_d, ssm_norm,
                                  w_br_s5, w_br_att, w_br_ssm, w_o, norm_mlp, w_up, w_down)))
    depth = w_in.shape[0]
    nb, lt, _ = x_prompt.shape
    ns, n_new, _ = x_sample.shape
    assert n_new == 1 and ns % SUBLANES == 0 and lt % (SUBLANES * Q_BLOCK) == 0
    past = page_table.shape[1] * PAGE_SIZE
    tab_p = _rope_table(jnp.arange(lt))
    tab_s = _rope_table(jnp.full((ns,), past, jnp.int32))
    xp = x_prompt.reshape(nb * lt, D_MODEL)
    xs = x_sample.reshape(ns, D_MODEL)
    st_p, st_s = [], []
    for l in range(depth):
        p = _layer_params({k: v[l] for k, v in weights.items()}, lt // SUBLANES)
        xp, sp = _layer_prompt(xp, p, tab_p, nb, lt)
        caches = (cache_k, cache_v, cache_idx_k, page_table, l)
        states = (state_s5_re[l], state_s5_im[l], state_conv[l], state_ssm[l])
        xs, ss = _layer_sample(xs, p, tab_s, caches, states)
        st_p.append(sp)
        st_s.append(ss)
    outs_p = [jnp.stack([s[i] for s in st_p]) for i in range(7)]
    outs_s = [jnp.stack([s[i] for s in st_s]) for i in range(7)]
    return (xp.reshape(nb, lt, D_MODEL), xs.reshape(ns, 1, D_MODEL), *outs_p, *outs_s)
```

```python
import functools
import math

import jax
import jax.numpy as jnp
from jax import lax
from jax.experimental import pallas as pl
from jax.experimental.pallas import tpu as pltpu

F32 = jnp.float32
BF16 = jnp.bfloat16
I32 = jnp.int32

D_MODEL = 1024
EPS = 1e-6
S5_GROUP = 16
S5_WIDTH = 512
S5_GROUPS = 32
S5_STATE = 64
S5_LANES = S5_GROUPS * S5_STATE
N_HEADS = 8
N_KV_HEADS = 4
HEAD_DIM = 64
ATT_WIDTH = 512
KV_WIDTH = 256
ROT_DIM = 16
ROPE_THETA = 500000.0
IDX_HEADS = 4
IDX_DIM = 64
IDX_SCALE = (IDX_HEADS * IDX_DIM) ** -0.5
TOPK_MAX = 256
Q_BLOCK = 128
PAGE_SIZE = 128
SSM_INNER = 1024
SSM_HEAD_DIM = 64
SSM_HEADS = 16
SSM_GROUPS = 2
SSM_STATE = 128
CONV_WIDTH = 4
CONV_DIM = 1536
SSD_CHUNK = 128
D_FF = 4096
IN_SIZES = (512, 512, 256, 256, 256, 64, 4, 1024, 1536, 16, 3072)

LANES = 128
SUBLANES = 8
VMEM_LIMIT = 56 * 1024 * 1024

NP = 7680
OFF_GATES, OFF_Z, OFF_U, OFF_XBC, OFF_Q, OFF_K, OFF_V, OFF_IQ, OFF_IKW, OFF_DT = (
    0, 3072, 4096, 4608, 6144, 6656, 6912, 7168, 7424, 7552)

NEG_BIG = -1e30
Q_SCALE = HEAD_DIM ** -0.5 * math.log2(math.e)
V_ROWS = HEAD_DIM + 16
INT_MIN = -(2 ** 31)


def _cparams(*sem):
    return pltpu.CompilerParams(dimension_semantics=sem, vmem_limit_bytes=VMEM_LIMIT)


def _split3(x):
    hi = x.astype(BF16)
    r = x - hi.astype(F32)
    mid = r.astype(BF16)
    lo = (r - mid.astype(F32)).astype(BF16)
    return hi, mid, lo


def _dot(a, b):
    return jnp.dot(a, b, preferred_element_type=F32)


def _dot_nt(a, b):
    return lax.dot_general(a, b, (((1,), (1,)), ((), ())), preferred_element_type=F32)


def _dot_tn(a, b):
    return lax.dot_general(a, b, (((0,), (0,)), ((), ())), preferred_element_type=F32)


def _dot3_exact_rhs(x, m_bf16):
    hi, mid, lo = _split3(x)
    return _dot(hi, m_bf16) + _dot(mid, m_bf16) + _dot(lo, m_bf16)


def _dot3_exact_lhs(m_bf16, x):
    hi, mid, lo = _split3(x)
    return _dot(m_bf16, hi) + _dot(m_bf16, mid) + _dot(m_bf16, lo)


def _inproj_kernel(x_ref, g_ref, w_ref, o_ref, xn_ref):
    @pl.when(pl.program_id(1) == 0)
    def _():
        x = x_ref[...]
        ms = jnp.mean(x * x, axis=-1, keepdims=True)
        xn_ref[...] = (x * lax.rsqrt(ms + EPS) * g_ref[...]).astype(BF16)

    o_ref[...] = _dot(xn_ref[...], w_ref[...])


def _inproj(x, g, wp, tm, tn=1536):
    m = x.shape[0]
    return pl.pallas_call(
        _inproj_kernel,
        grid=(m // tm, NP // tn),
        in_specs=[
            pl.BlockSpec((tm, D_MODEL), lambda i, j: (i, 0)),
            pl.BlockSpec((1, D_MODEL), lambda i, j: (0, 0)),
            pl.BlockSpec((D_MODEL, tn), lambda i, j: (0, j)),
        ],
        out_specs=pl.BlockSpec((tm, tn), lambda i, j: (i, j)),
        out_shape=jax.ShapeDtypeStruct((m, NP), F32),
        scratch_shapes=[pltpu.VMEM((tm, D_MODEL), BF16)],
        compiler_params=_cparams("parallel", "arbitrary"),
        name="inproj",
    )(x, g, wp)


def _rope(x, c, sa, sb):
    outs = []
    for t in range(x.shape[1] // LANES):
        xt = x[:, t * LANES:(t + 1) * LANES]
        outs.append(xt * c + pltpu.roll(xt, LANES - ROT_DIM // 2, 1) * sa + pltpu.roll(xt, ROT_DIM // 2, 1) * sb)
    return outs[0] if len(outs) == 1 else jnp.concatenate(outs, axis=1)


def _head_rms(x, bd_ref, gain):
    ms = _dot3_exact_rhs(x * x, bd_ref[...]) * (1.0 / HEAD_DIM)
    return x * lax.rsqrt(ms + EPS) * gain


def _post_kernel(q_ref, k_ref, v_ref, iq_ref, ikw_ref, tab_ref, qg_ref, kg_ref, bdq_ref, bdk_ref,
                 qb_ref, ko_ref, kb_ref, vo_ref, vb_ref, iqh_ref, iql_ref, iko_ref, ikc_ref,
                 *, v_transposed):
    c = tab_ref[:, 0:LANES]
    sa = tab_ref[:, LANES:2 * LANES]
    sb = tab_ref[:, 2 * LANES:3 * LANES]
    q = _rope(_head_rms(q_ref[...], bdq_ref, qg_ref[...]), c, sa, sb)
    qb_ref[...] = (q * Q_SCALE).astype(BF16)
    k = _rope(_head_rms(k_ref[...], bdk_ref, kg_ref[...]), c, sa, sb)
    for g in range(N_KV_HEADS):
        ko_ref[:, g, :] = k[:, g * HEAD_DIM:(g + 1) * HEAD_DIM]
    kb_ref[...] = k.astype(BF16)
    v = v_ref[...]
    for g in range(N_KV_HEADS):
        vo_ref[:, g, :] = v[:, g * HEAD_DIM:(g + 1) * HEAD_DIM]
    if v_transposed:
        v_t = v.T.astype(BF16)
        ones = jnp.ones((V_ROWS - HEAD_DIM, v.shape[0]), BF16)
        vb_ref[...] = jnp.concatenate(
            [blk for g in range(N_KV_HEADS) for blk in (v_t[g * HEAD_DIM:(g + 1) * HEAD_DIM, :], ones)], axis=0)
    else:
        vb_ref[...] = v.astype(BF16)
    iq = _rope(iq_ref[...], c, sa, sb)
    iqh = iq.astype(BF16)
    iqh_ref[...] = iqh
    iql_ref[...] = (iq - iqh.astype(F32)).astype(BF16)
    ikr = _rope(ikw_ref[...], c, sa, sb)
    iko_ref[...] = ikr[:, 0:IDX_DIM]
    lane = lax.broadcasted_iota(I32, ikr.shape, 1)
    ik = jnp.where(lane < IDX_DIM, ikr, 0.0)
    ikh = ik.astype(BF16)
    ikl = pltpu.roll(ik - ikh.astype(F32), IDX_DIM, 1).astype(BF16)
    ikc_ref[...] = jnp.concatenate([ikh + ikl, ikh], axis=1)


def _post(h, tab, qg, kg, bdq, bdk, nb, lt, tt, v_transposed):
    m = nb * lt
    nt = lt // tt

    def col(width, off):
        return pl.BlockSpec((tt, width), lambda b, i: (b * nt + i, off // width))

    def out(width):
        return pl.BlockSpec((tt, width), lambda b, i: (b * nt + i, 0))

    def full(shape):
        return pl.BlockSpec(shape, lambda b, i: (0, 0))

    shapes = [
        (ATT_WIDTH, BF16), (KV_WIDTH, F32), (KV_WIDTH, BF16), (KV_WIDTH, F32), (KV_WIDTH, BF16),
        (KV_WIDTH, BF16), (KV_WIDTH, BF16), (IDX_DIM, F32), (KV_WIDTH, BF16)]
    out_specs = [out(w) for w, _ in shapes]
    out_shape = [jax.ShapeDtypeStruct((m, w), d) for w, d in shapes]
    for idx in (1, 3):
        out_specs[idx] = pl.BlockSpec((tt, N_KV_HEADS, HEAD_DIM), lambda b, i: (b * nt + i, 0, 0))
        out_shape[idx] = jax.ShapeDtypeStruct((m, N_KV_HEADS, HEAD_DIM), F32)
    if v_transposed:
        out_specs[4] = pl.BlockSpec((None, N_KV_HEADS * V_ROWS, tt), lambda b, i: (b * nt + i, 0, 0))
        out_shape[4] = jax.ShapeDtypeStruct((m // tt, N_KV_HEADS * V_ROWS, tt), BF16)
    return pl.pallas_call(
        functools.partial(_post_kernel, v_transposed=v_transposed),
        grid=(nb, nt),
        in_specs=[
            col(ATT_WIDTH, OFF_Q), col(KV_WIDTH, OFF_K), col(KV_WIDTH, OFF_V), col(KV_WIDTH, OFF_IQ),
            col(LANES, OFF_IKW),
            pl.BlockSpec((tt, 3 * LANES), lambda b, i: (i, 0)),
            full((1, ATT_WIDTH)), full((1, KV_WIDTH)), full((ATT_WIDTH, ATT_WIDTH)), full((KV_WIDTH, KV_WIDTH)),
        ],
        out_specs=out_specs,
        out_shape=out_shape,
        compiler_params=_cparams("parallel", "parallel"),
        name="qk_post",
    )(h, h, h, h, h, tab, qg, kg, bdq, bdk)


def _sort_key(score):
    score = jnp.where(score == 0.0, 0.0, score)
    bits = pltpu.bitcast(score, I32)
    return bits ^ (lax.shift_right_arithmetic(bits, 31) & 0x7FFFFFFF)


def _topk_select_single(keys, k_sel, idx_bits):
    def count(hit):
        return jnp.sum(jnp.sum(hit.astype(I32), axis=0, keepdims=True), axis=1, keepdims=True)

    def bit_body(i, carry):
        t_u, cnt_t = carry
        cand_u = t_u | lax.shift_left(jnp.int32(1), 31 - i)
        cnt = count(keys >= (cand_u ^ INT_MIN))
        take = cnt >= k_sel
        return jnp.where(take, cand_u, t_u), jnp.where(take, cnt, cnt_t)

    init = (jnp.zeros((1, 1), I32), jnp.full((1, 1), keys.shape[0] * LANES, I32))
    t_u, cnt_t = lax.fori_loop(0, 32, bit_body, init)
    t = t_u ^ INT_MIN
    pos = (lax.broadcasted_iota(I32, keys.shape, 0) * LANES + lax.broadcasted_iota(I32, keys.shape, 1))

    def tie_search(_):
        need_m1 = k_sel - 1 - count(keys > t)

        def jbody(i, j):
            cand = j | lax.shift_left(jnp.int32(1), idx_bits - 1 - i)
            f = count((keys == t) & (pos < cand))
            return jnp.where(f <= need_m1, cand, j)

        return lax.fori_loop(0, idx_bits, jbody, jnp.zeros((1, 1), I32))

    jcut = lax.cond(jnp.max(cnt_t) > k_sel, tie_search, lambda _: jnp.full((1, 1), 2 ** 30, I32), 0)
    return (keys > t) | ((keys == t) & (pos <= jcut))


NEG_INF_KEY = -2139095041
ATT_SLAB = 128


def _key_byte(kk, stage):
    part = lax.shift_right_arithmetic(kk, 24 - 8 * stage)
    return part + 128 if stage == 0 else part & 0xFF


def _topk_threshold_t(key_ref, l_ref, nkc, kc, k_sel, idx_bits):
    one_b = jnp.ones((kc, LANES), BF16)
    zero_b = jnp.zeros((kc, LANES), BF16)
    as_b = lambda u: u.astype(F32).astype(BF16)
    pack = 16

    def count8(cand, strict):
        def body(c, acc):
            x = l_ref[c]
            hit = jnp.where((x > cand) if strict else (x >= cand), one_b, zero_b)
            parts = [hit[i * pack:(i + 1) * pack, :] for i in range(kc // pack)]
            while len(parts) > 1:
                parts = [a + b for a, b in zip(parts[0::2], parts[1::2])]
            return acc + parts[0].astype(F32)
        acc = lax.fori_loop(0, nkc, body, jnp.zeros((pack, LANES), F32))
        return jnp.sum(acc, axis=0, keepdims=True)

    def search_byte(k_need):
        def body(i, t_u):
            cand_u = t_u | lax.shift_left(jnp.int32(1), 7 - i)
            return jnp.where(count8(as_b(cand_u), False) >= k_need, cand_u, t_u)
        return lax.fori_loop(0, 8, body, jnp.zeros((1, LANES), I32))

    def refill(stage, prefix):
        def body(c, _):
            kk = key_ref[c]
            match = lax.shift_right_arithmetic(kk, 32 - 8 * stage) == prefix
            l_ref[c] = jnp.where(match, _key_byte(kk, stage).astype(F32), -1.0).astype(BF16)
            return 0
        lax.fori_loop(0, nkc, body, 0)

    k_need = jnp.full((1, LANES), k_sel, F32)
    t = None
    for stage in range(4):
        if stage > 0:
            refill(stage, t)
        tb = search_byte(k_need)
        if stage < 3:
            k_need = k_need - count8(as_b(tb), True)
        t = tb - 128 if stage == 0 else lax.shift_left(t, 8) | tb
    cnt_t = (k_sel - k_need) + count8(as_b(tb), False)

    def count(pred):
        def body(c, acc):
            hit = pred(key_ref[c], c).astype(I32)
            return acc + jnp.sum(hit.reshape(kc // SUBLANES, SUBLANES, LANES), axis=0)
        acc = lax.fori_loop(0, nkc, body, jnp.zeros((SUBLANES, LANES), I32))
        return jnp.sum(acc, axis=0, keepdims=True)

    @pl.when(jnp.max(cnt_t) > k_sel)
    def _():
        need_m1 = k_sel - 1 - count(lambda kk, c: kk > t)

        def pos_of(c):
            return c * kc + lax.broadcasted_iota(I32, (kc, LANES), 0)

        def jbody(i, j):
            cand = j | lax.shift_left(jnp.int32(1), idx_bits - 1 - i)
            f = count(lambda kk, c: (kk == t) & (pos_of(c) < cand))
            return jnp.where(f <= need_m1, cand, j)

        jcut = lax.fori_loop(0, idx_bits, jbody, jnp.zeros((1, LANES), I32))

        def demote(c, _):
            kk = key_ref[c]
            key_ref[c] = jnp.where((kk == t) & (pos_of(c) > jcut), t - 1, kk)
            return 0

        lax.fori_loop(0, nkc, demote, 0)

    return jnp.maximum(t, NEG_INF_KEY + 1)


def _dsa_kernel(qb_ref, iqh_ref, iql_ref, ikw_ref, kb_ref, vt_ref, ikc_ref, o_ref,
                key_ref, l_ref, wq_ref, wi_ref, s_ref, bias_ref, m_ref, acc_ref, *, kc, k_sel, idx_bits):
    qi = pl.program_id(1)
    nkc = (qi * Q_BLOCK) // kc + 1
    qpos = qi * Q_BLOCK + lax.broadcasted_iota(I32, (1, Q_BLOCK), 1)

    halves = []
    for ref in (iqh_ref, iql_ref):
        iq_t = ref[...].astype(F32).T
        halves.append(jnp.concatenate(
            [iq_t[h * IDX_DIM:(h + 1) * IDX_DIM, :] for h in range(IDX_HEADS)], axis=1).astype(BF16))
    wi_ref[...] = jnp.concatenate([halves[0], halves[0], halves[1], jnp.zeros_like(halves[0])], axis=0)
    iw_t = ikw_ref[...].T
    q_t = qb_ref[...].astype(F32).T
    zrow = jnp.zeros((HEAD_DIM, 2 * Q_BLOCK), F32)
    for g in range(N_KV_HEADS):
        blk = jnp.concatenate([q_t[(2 * g + r) * HEAD_DIM:(2 * g + r + 1) * HEAD_DIM, :] for r in range(2)], axis=1)
        rows = [blk if gg == g else zrow for gg in range(N_KV_HEADS)]
        wq_ref[g] = jnp.concatenate(rows, axis=0).astype(BF16)

    def idx_body(c, _):
        r0 = pl.multiple_of(c * kc, kc)
        d = jnp.maximum(_dot(ikc_ref[pl.ds(r0, kc), :], wi_ref[...]), 0.0)
        sc = iw_t[IDX_DIM:IDX_DIM + 1, :] * d[:, 0:Q_BLOCK]
        for h in range(1, IDX_HEADS):
            sc = sc + iw_t[IDX_DIM + h:IDX_DIM + h + 1, :] * d[:, h * Q_BLOCK:(h + 1) * Q_BLOCK]
        kpos = c * kc + lax.broadcasted_iota(I32, (kc, Q_BLOCK), 0)
        sc = jnp.where(kpos <= qpos, sc * IDX_SCALE, -jnp.inf)
        key = _sort_key(sc)
        key_ref[c] = key
        l_ref[c] = _key_byte(key, 0).astype(F32).astype(BF16)
        return 0

    lax.fori_loop(0, nkc, idx_body, 0)
    t = _topk_threshold_t(key_ref, l_ref, nkc, kc, k_sel, idx_bits)

    m_ref[...] = jnp.full(m_ref.shape, NEG_BIG, F32)
    acc_ref[...] = jnp.zeros(acc_ref.shape, F32)

    def bias_body(c, _):
        bias_ref[c] = jnp.where(key_ref[c] >= t, 0.0, NEG_BIG)
        return 0

    lax.fori_loop(0, nkc, bias_body, 0)

    def att_body(c, _):
        r0 = pl.multiple_of(c * kc, kc)
        alphas = []
        for g in range(N_KV_HEADS):
            cmax = None
            for rb in range(kc // ATT_SLAB):
                rows = slice(rb * ATT_SLAB, (rb + 1) * ATT_SLAB)
                bias = bias_ref[c, rows, :]
                kslab = kb_ref[pl.ds(r0 + rb * ATT_SLAB, ATT_SLAB), :]
                s = _dot(kslab, wq_ref[g]) + jnp.concatenate([bias, bias], axis=1)
                s_ref[g, rows, :] = s
                smax = jnp.max(s, axis=0, keepdims=True)
                cmax = smax if cmax is None else jnp.maximum(cmax, smax)
            m_old = m_ref[g]
            m_new = jnp.maximum(m_old, cmax)
            m_ref[g] = m_new
            alphas.append(jnp.exp2(m_old - m_new))
        for g in range(N_KV_HEADS):
            p = jnp.exp2(s_ref[g] - m_ref[g]).astype(BF16)
            vt = vt_ref[c, g * V_ROWS:(g + 1) * V_ROWS, :]
            acc_ref[g] = alphas[g] * acc_ref[g] + _dot(vt, p)
        return 0

    lax.fori_loop(0, nkc, att_body, 0)

    res_t = jnp.concatenate(
        [acc_ref[g, 0:HEAD_DIM, :] / acc_ref[g, HEAD_DIM:HEAD_DIM + 1, :] for g in range(N_KV_HEADS)],
        axis=0)
    res = res_t.T
    o_ref[...] = jnp.concatenate([res[0:Q_BLOCK], res[Q_BLOCK:]], axis=1)


def _dsa_prompt(qb, iqh, iql, h, kb, vt, ikc, nb, lt, kc):
    nq = lt // Q_BLOCK
    nck = lt // kc
    k_sel = min(TOPK_MAX, lt // 4)
    idx_bits = max(1, (lt - 1).bit_length())

    def qblk(width, off=0):
        return pl.BlockSpec((Q_BLOCK, width), lambda b, i: (b * nq + i, off // width))

    def seq(width):
        return pl.BlockSpec((lt, width), lambda b, i: (b, 0))

    return pl.pallas_call(
        functools.partial(_dsa_kernel, kc=kc, k_sel=k_sel, idx_bits=idx_bits),
        grid=(nb, nq),
        in_specs=[qblk(ATT_WIDTH), qblk(KV_WIDTH), qblk(KV_WIDTH), qblk(LANES, OFF_IKW),
                  seq(KV_WIDTH), pl.BlockSpec((nck, N_KV_HEADS * V_ROWS, kc), lambda b, i: (b, 0, 0)),
                  seq(KV_WIDTH)],
        out_specs=qblk(ATT_WIDTH),
        out_shape=jax.ShapeDtypeStruct((nb * lt, ATT_WIDTH), F32),
        scratch_shapes=[
            pltpu.VMEM((nck, kc, Q_BLOCK), I32),
            pltpu.VMEM((nck, kc, Q_BLOCK), BF16),
            pltpu.VMEM((N_KV_HEADS, KV_WIDTH, 2 * Q_BLOCK), BF16),
            pltpu.VMEM((4 * IDX_DIM, IDX_HEADS * Q_BLOCK), BF16),
            pltpu.VMEM((N_KV_HEADS, kc, 2 * Q_BLOCK), F32),
            pltpu.VMEM((nck, kc, Q_BLOCK), F32),
            pltpu.VMEM((N_KV_HEADS, 1, 2 * Q_BLOCK), F32),
            pltpu.VMEM((N_KV_HEADS, V_ROWS, 2 * Q_BLOCK), F32),
        ],
        compiler_params=_cparams("parallel", "arbitrary"),
        name="dsa_prompt",
    )(qb, iqh, iql, h, kb, vt, ikc)


S5_LANE_BLOCK = 512


S5_TILES = S5_WIDTH // LANES


def _s5_permute_in(u_ref, up_ref, tt):
    if tt == 1:
        return u_ref[:, 0, :]
    for j in range(SUBLANES):
        uj = u_ref[j]
        for k in range(S5_TILES):
            up_ref[k, pl.ds(j, tt, stride=SUBLANES), :] = uj[:, k * LANES:(k + 1) * LANES]
    return jnp.concatenate([up_ref[k] for k in range(S5_TILES)], axis=1)


def _s5_permute_out(res, y_ref, o_ref, tt):
    if tt == 1:
        o_ref[:, 0, :] = res
        return
    for k in range(S5_TILES):
        y_ref[k] = res[:, k * LANES:(k + 1) * LANES]
    for j in range(SUBLANES):
        o_ref[j] = jnp.concatenate(
            [y_ref[k, pl.ds(j, tt, stride=SUBLANES), :] for k in range(S5_TILES)], axis=1)


def _s5_scan(bu_ref, ar_ref, ai_ref, hr_ref, hi_ref, tt, store):
    for lb in range(S5_LANES // S5_LANE_BLOCK):
        re = slice(lb * S5_LANE_BLOCK, (lb + 1) * S5_LANE_BLOCK)
        im = slice(S5_LANES + lb * S5_LANE_BLOCK, S5_LANES + (lb + 1) * S5_LANE_BLOCK)
        ar = jnp.broadcast_to(ar_ref[:, re], (SUBLANES, S5_LANE_BLOCK))
        ai = jnp.broadcast_to(ai_ref[:, re], (SUBLANES, S5_LANE_BLOCK))

        def step(t, carry):
            hr, hi = carry
            r0 = pl.multiple_of(t * SUBLANES, SUBLANES)
            nhr = ar * hr - ai * hi + bu_ref[pl.ds(r0, SUBLANES), re]
            nhi = ar * hi + ai * hr + bu_ref[pl.ds(r0, SUBLANES), im]
            if store:
                bu_ref[pl.ds(r0, SUBLANES), re] = nhr
                bu_ref[pl.ds(r0, SUBLANES), im] = nhi
            return nhr, nhi

        hr, hi = lax.fori_loop(0, tt, step, (hr_ref[:, re], hi_ref[:, re]), unroll=min(tt, 8))
        hr_ref[:, re] = hr
        hi_ref[:, re] = hi


def _s5_ends_kernel(u_ref, bmat_ref, ar_ref, ai_ref, e_ref, up_ref, bu_ref, hr_ref, hi_ref, *, tt):
    i = pl.program_id(1)

    @pl.when(i == 0)
    def _():
        hr_ref[...] = jnp.zeros(hr_ref.shape, F32)
        hi_ref[...] = jnp.zeros(hi_ref.shape, F32)

    up = _s5_permute_in(u_ref, up_ref, tt)
    bu_ref[...] = _dot(up.astype(BF16), bmat_ref[...])
    _s5_scan(bu_ref, ar_ref, ai_ref, hr_ref, hi_ref, tt, store=False)

    @pl.when(i == pl.num_programs(1) - 1)
    def _():
        e_ref[:, 0:S5_LANES] = hr_ref[...]
        e_ref[:, S5_LANES:] = hi_ref[...]


def _s5_carry_kernel(e_ref, pr_ref, pi_ref, h0_ref):
    e = e_ref[...]
    pr = pr_ref[...]
    pi = pi_ref[...]
    cr = jnp.zeros((1, S5_LANES), F32)
    ci = jnp.zeros((1, S5_LANES), F32)
    for j in range(SUBLANES):
        h0_ref[j:j + 1, 0:S5_LANES] = cr
        h0_ref[j:j + 1, S5_LANES:] = ci
        er = e[j:j + 1, 0:S5_LANES]
        ei = e[j:j + 1, S5_LANES:]
        cr, ci = pr * cr - pi * ci + er, pr * ci + pi * cr + ei


def _s5_main_kernel(u_ref, h0_ref, bmat_ref, cmat_ref, ar_ref, ai_ref, d_ref, wglu_ref, o_ref, hl_ref,
                    up_ref, bu_ref, y_ref, hr_ref, hi_ref, *, tt):
    i = pl.program_id(1)

    @pl.when(i == 0)
    def _():
        hr_ref[...] = h0_ref[:, 0:S5_LANES]
        hi_ref[...] = h0_ref[:, S5_LANES:]

    up = _s5_permute_in(u_ref, up_ref, tt)
    bu_ref[...] = _dot(up.astype(BF16), bmat_ref[...])
    _s5_scan(bu_ref, ar_ref, ai_ref, hr_ref, hi_ref, tt, store=True)

    @pl.when(i == pl.num_programs(1) - 1)
    def _():
        hl_ref[:, 0:S5_LANES] = hr_ref[...]
        hl_ref[:, S5_LANES:] = hi_ref[...]

    y = _dot(bu_ref[...].astype(BF16), cmat_ref[...]) + d_ref[...] * up
    g = jax.nn.gelu(y)
    res = g * jax.nn.sigmoid(_dot(g.astype(BF16), wglu_ref[...]))
    _s5_permute_out(res, y_ref, o_ref, tt)


def _s5_specs(nb, lseg, tt, col):
    u_spec = pl.BlockSpec((None, SUBLANES, tt, S5_WIDTH), lambda b, i: (b, 0, i, col))
    st_spec = pl.BlockSpec((None, SUBLANES, 2 * S5_LANES), lambda b, i: (b, 0, 0))

    def full(shape):
        return pl.BlockSpec(shape, lambda b, i: (0,) * len(shape))

    return u_spec, st_spec, full


def _s5_ends(u4, col, bmat, ar, ai, tt):
    nb, _, lseg, _ = u4.shape
    u_spec, st_spec, full = _s5_specs(nb, lseg, tt, col)
    return pl.pallas_call(
        functools.partial(_s5_ends_kernel, tt=tt),
        grid=(nb, lseg // tt),
        in_specs=[u_spec, full((S5_WIDTH, 2 * S5_LANES)), full((1, S5_LANES)), full((1, S5_LANES))],
        out_specs=st_spec,
        out_shape=jax.ShapeDtypeStruct((nb, SUBLANES, 2 * S5_LANES), F32),
        scratch_shapes=[
            pltpu.VMEM((S5_TILES, tt * SUBLANES, LANES), F32),
            pltpu.VMEM((tt * SUBLANES, 2 * S5_LANES), F32),
            pltpu.VMEM((SUBLANES, S5_LANES), F32),
            pltpu.VMEM((SUBLANES, S5_LANES), F32),
        ],
        compiler_params=_cparams("parallel", "arbitrary"),
        name="s5_ends",
    )(u4, bmat, ar, ai)


def _s5_carry(e, pr, pi):
    nb = e.shape[0]
    st_spec = pl.BlockSpec((None, SUBLANES, 2 * S5_LANES), lambda b: (b, 0, 0))
    p_spec = pl.BlockSpec((1, S5_LANES), lambda b: (0, 0))
    return pl.pallas_call(
        _s5_carry_kernel,
        grid=(nb,),
        in_specs=[st_spec, p_spec, p_spec],
        out_specs=st_spec,
        out_shape=jax.ShapeDtypeStruct(e.shape, F32),
        compiler_params=_cparams("parallel"),
        name="s5_carry",
    )(e, pr, pi)


def _s5_main(u4, col, h0, bmat, cmat, ar, ai, dskip, wglu, tt):
    nb, _, lseg, _ = u4.shape
    u_spec, st_spec, full = _s5_specs(nb, lseg, tt, col)
    o_spec = pl.BlockSpec((None, SUBLANES, tt, S5_WIDTH), lambda b, i: (b, 0, i, 0))
    return pl.pallas_call(
        functools.partial(_s5_main_kernel, tt=tt),
        grid=(nb, lseg // tt),
        in_specs=[u_spec, st_spec, full((S5_WIDTH, 2 * S5_LANES)), full((2 * S5_LANES, S5_WIDTH)),
                  full((1, S5_LANES)), full((1, S5_LANES)), full((1, S5_WIDTH)), full((S5_WIDTH, S5_WIDTH))],
        out_specs=[o_spec, st_spec],
        out_shape=[jax.ShapeDtypeStruct((nb, SUBLANES, lseg, S5_WIDTH), F32),
                   jax.ShapeDtypeStruct((nb, SUBLANES, 2 * S5_LANES), F32)],
        scratch_shapes=[
            pltpu.VMEM((S5_TILES, tt * SUBLANES, LANES), F32),
            pltpu.VMEM((tt * SUBLANES, 2 * S5_LANES), F32),
            pltpu.VMEM((S5_TILES, tt * SUBLANES, LANES), F32),
            pltpu.VMEM((SUBLANES, S5_LANES), F32),
            pltpu.VMEM((SUBLANES, S5_LANES), F32),
        ],
        compiler_params=_cparams("parallel", "arbitrary"),
        name="s5_main",
    )(u4, h0, bmat, cmat, ar, ai, dskip, wglu)


def _softplus(x):
    return jnp.maximum(x, 0.0) + jnp.log1p(jnp.exp(-jnp.abs(x)))


def _silu(x):
    return x * jax.nn.sigmoid(x)


def _ssd_gate_norm(y, z, nrm):
    y = y * _silu(z)
    outs = []
    gw = SSM_INNER // SSM_GROUPS
    for g in range(SSM_GROUPS):
        yg = y[:, g * gw:(g + 1) * gw]
        ms = jnp.mean(yg * yg, axis=-1, keepdims=True)
        outs.append(yg * lax.rsqrt(ms + EPS))
    return jnp.concatenate(outs, axis=1) * nrm


def _ssd_kernel(xbc_ref, z_ref, dt_ref, cw_ref, cb_ref, dtb_ref, alog_ref, dexp_ref, nrm_ref, exp_ref,
                o_ref, st_ref, xpad_ref, stt_ref):
    c = pl.program_id(1)
    t = SSD_CHUNK
    gw = SSM_INNER // SSM_GROUPS

    @pl.when(c == 0)
    def _():
        xpad_ref[0:SUBLANES, :] = jnp.zeros((SUBLANES, CONV_DIM), F32)
        stt_ref[...] = jnp.zeros(stt_ref.shape, F32)

    x = xbc_ref[...]
    xpad_ref[SUBLANES:SUBLANES + t, :] = x
    conv = cb_ref[...]
    for k in range(CONV_WIDTH):
        conv = conv + xpad_ref[pl.ds(SUBLANES - (CONV_WIDTH - 1) + k, t), :] * cw_ref[k:k + 1, :]
    xpad_ref[0:SUBLANES, :] = x[t - SUBLANES:t, :]
    act = _silu(conv)
    xs = act[:, 0:SSM_INNER]
    bm = act[:, SSM_INNER:SSM_INNER + SSM_GROUPS * SSM_STATE]
    cm = act[:, SSM_INNER + SSM_GROUPS * SSM_STATE:]

    dt = _softplus(dt_ref[...] + dtb_ref[...])
    a = -jnp.exp(alog_ref[...])
    row = lax.broadcasted_iota(I32, (t, t), 0)
    colm = lax.broadcasted_iota(I32, (t, t), 1)
    causal = row >= colm
    ltri = jnp.where(causal, 1.0, 0.0).astype(BF16)
    cs = _dot3_exact_lhs(ltri, dt * a)
    cs_t = cs.T
    dt_t = dt.T
    ecs = jnp.exp(cs)
    wgt = jnp.exp(cs[t - 1:t, :] - cs) * dt
    expand = exp_ref[...]
    ecs_x = _dot3_exact_rhs(ecs, expand)
    wgt_x = _dot3_exact_rhs(wgt, expand)

    lane = lax.broadcasted_iota(I32, (t, LANES), 1)
    lo_half = lane < SSM_HEAD_DIM
    xs_b = xs.astype(BF16)
    zero_b = jnp.zeros((t, LANES), BF16)
    y_tiles = []
    for hp in range(SSM_HEADS // 2):
        g = hp // (SSM_HEADS // 2 // SSM_GROUPS)
        cb = _dot_nt(cm[:, g * SSM_STATE:(g + 1) * SSM_STATE].astype(BF16),
                     bm[:, g * SSM_STATE:(g + 1) * SSM_STATE].astype(BF16))
        xt = xs_b[:, hp * LANES:(hp + 1) * LANES]
        acc = None
        for hh in range(2):
            h = 2 * hp + hh
            seg = cs[:, h:h + 1] - cs_t[h:h + 1, :]
            dec = jnp.exp(jnp.where(causal, seg, -jnp.inf))
            w = (cb * dec * dt_t[h:h + 1, :]).astype(BF16)
            xm = jnp.where(lo_half if hh == 0 else ~lo_half, xt, zero_b)
            part = _dot(w, xm)
            acc = part if acc is None else acc + part
        y_tiles.append(acc)
    y = jnp.concatenate(y_tiles, axis=1)

    xw = (xs * wgt_x).astype(BF16)
    cdec = ecs_x[t - 1:t, :]
    y_off = []
    for g in range(SSM_GROUPS):
        st = stt_ref[g]
        cg = cm[:, g * SSM_STATE:(g + 1) * SSM_STATE].astype(BF16)
        y_off.append(_dot(cg, st.astype(BF16)) * ecs_x[:, g * gw:(g + 1) * gw])
        bg = bm[:, g * SSM_STATE:(g + 1) * SSM_STATE].astype(BF16)
        stt_ref[g] = st * cdec[:, g * gw:(g + 1) * gw] + _dot_tn(bg, xw[:, g * gw:(g + 1) * gw])
    y = y + jnp.concatenate(y_off, axis=1) + dexp_ref[...] * xs
    o_ref[...] = _ssd_gate_norm(y, z_ref[...], nrm_ref[...])

    @pl.when(c == pl.num_programs(1) - 1)
    def _():
        for g in range(SSM_GROUPS):
            st_ref[g * gw:(g + 1) * gw, :] = stt_ref[g].T


def _ssd_prompt(h, cw, cb, dtb, alog, dexp, nrm, expand, nb, lt):
    nc = lt // SSD_CHUNK

    def col(width, off):
        return pl.BlockSpec((SSD_CHUNK, width), lambda b, c: (b * nc + c, off // width))

    def full(shape):
        return pl.BlockSpec(shape, lambda b, c: (0, 0))

    return pl.pallas_call(
        _ssd_kernel,
        grid=(nb, nc),
        in_specs=[col(CONV_DIM, OFF_XBC), col(SSM_INNER, OFF_Z), col(LANES, OFF_DT),
                  full((CONV_WIDTH, CONV_DIM)), full((1, CONV_DIM)), full((1, LANES)), full((1, LANES)),
                  full((1, SSM_INNER)), full((1, SSM_INNER)), full((LANES, SSM_INNER))],
        out_specs=[pl.BlockSpec((SSD_CHUNK, SSM_INNER), lambda b, c: (b * nc + c, 0)),
                   pl.BlockSpec((None, SSM_INNER, SSM_STATE), lambda b, c: (b, 0, 0))],
        out_shape=[jax.ShapeDtypeStruct((nb * lt, SSM_INNER), F32),
                   jax.ShapeDtypeStruct((nb, SSM_INNER, SSM_STATE), F32)],
        scratch_shapes=[pltpu.VMEM((SUBLANES + SSD_CHUNK, CONV_DIM), F32),
                        pltpu.VMEM((SSM_GROUPS, SSM_STATE, SSM_INNER // SSM_GROUPS), F32)],
        compiler_params=_cparams("parallel", "arbitrary"),
        name="ssd_prompt",
    )(h, h, h, cw, cb, dtb, alog, dexp, nrm, expand)


def _merge_kernel(x_ref, g0_ref, g1_ref, g2_ref, os5_ref, oatt_ref, ossm_ref, w1_ref, w2_ref, w3_ref, wo_ref,
                  o_ref):
    merged = (jax.nn.sigmoid(g0_ref[...]) * _dot(os5_ref[...].astype(BF16), w1_ref[...])
              + jax.nn.sigmoid(g1_ref[...]) * _dot(oatt_ref[...].astype(BF16), w2_ref[...])
              + jax.nn.sigmoid(g2_ref[...]) * _dot(ossm_ref[...].astype(BF16), w3_ref[...]))
    o_ref[...] = x_ref[...] + _dot(merged.astype(BF16), wo_ref[...])


def _merge(x, h, os5, oatt, ossm, w1, w2, w3, wo, tm):
    m = x.shape[0]

    def rows(width, blk=0):
        return pl.BlockSpec((tm, width), lambda i: (i, blk))

    def full(shape):
        return pl.BlockSpec(shape, lambda i: (0, 0))

    return pl.pallas_call(
        _merge_kernel,
        grid=(m // tm,),
        in_specs=[rows(D_MODEL), rows(D_MODEL, 0), rows(D_MODEL, 1), rows(D_MODEL, 2),
                  rows(S5_WIDTH), rows(ATT_WIDTH), rows(SSM_INNER),
                  full((S5_WIDTH, D_MODEL)), full((ATT_WIDTH, D_MODEL)), full((SSM_INNER, D_MODEL)),
                  full((D_MODEL, D_MODEL))],
        out_specs=rows(D_MODEL),
        out_shape=jax.ShapeDtypeStruct((m, D_MODEL), F32),
        compiler_params=_cparams("parallel"),
        name="merge",
    )(x, h, h, h, os5, oatt, ossm, w1, w2, w3, wo)


def _mlp_kernel(x_ref, g_ref, wu_ref, wd_ref, o_ref):
    x = x_ref[...]
    ms = jnp.mean(x * x, axis=-1, keepdims=True)
    xn = (x * lax.rsqrt(ms + EPS) * g_ref[...]).astype(BF16)
    hid = jnp.maximum(_dot(xn, wu_ref[...]), 0.0)
    o_ref[...] = x + _dot((hid * hid).astype(BF16), wd_ref[...])


def _mlp(x, g, wu, wd, tm):
    m = x.shape[0]
    return pl.pallas_call(
        _mlp_kernel,
        grid=(m // tm,),
        in_specs=[pl.BlockSpec((tm, D_MODEL), lambda i: (i, 0)),
                  pl.BlockSpec((1, D_MODEL), lambda i: (0, 0)),
                  pl.BlockSpec((D_MODEL, D_FF), lambda i: (0, 0)),
                  pl.BlockSpec((D_FF, D_MODEL), lambda i: (0, 0))],
        out_specs=pl.BlockSpec((tm, D_MODEL), lambda i: (i, 0)),
        out_shape=jax.ShapeDtypeStruct((m, D_MODEL), F32),
        compiler_params=_cparams("parallel"),
        name="mlp",
    )(x, g, wu, wd)


def _ssd_step_kernel(xbc_ref, z_ref, dt_ref, cst_ref, st_ref, cw_ref, cb_ref, dtb_ref, alog_ref, dexp_ref,
                     nrm_ref, exp_ref, o_ref, cst_out_ref, st_out_ref):
    ns = SUBLANES
    gw = SSM_INNER // SSM_GROUPS
    x = xbc_ref[...]
    conv = cb_ref[...]
    for k in range(CONV_WIDTH - 1):
        conv = conv + cst_ref[k] * cw_ref[k:k + 1, :]
    conv = conv + x * cw_ref[CONV_WIDTH - 1:CONV_WIDTH, :]
    for k in range(CONV_WIDTH - 2):
        cst_out_ref[k] = cst_ref[k + 1]
    cst_out_ref[CONV_WIDTH - 2] = x
    act = _silu(conv)
    xs = act[:, 0:SSM_INNER]
    bm = act[:, SSM_INNER:SSM_INNER + SSM_GROUPS * SSM_STATE]
    cm = act[:, SSM_INNER + SSM_GROUPS * SSM_STATE:]
    dt = _softplus(dt_ref[...] + dtb_ref[...])
    decay = jnp.exp(dt * (-jnp.exp(alog_ref[...])))
    expand = exp_ref[...]
    decay_x = _dot3_exact_rhs(decay, expand)
    xdt = xs * _dot3_exact_rhs(dt, expand)
    rowid = lax.broadcasted_iota(I32, (ns, 1), 0)
    ones = jnp.ones((ns, SSM_STATE), BF16)
    y = jnp.zeros((ns, SSM_INNER), F32)
    y_parts = [[], []]
    for s in range(ns):
        pick = rowid == s
        for g in range(SSM_GROUPS):
            cols = slice(g * gw, (g + 1) * gw)
            nsl = slice(g * SSM_STATE, (g + 1) * SSM_STATE)
            dsel = jnp.where(pick, decay_x[:, cols], 0.0)
            dh, dm, dl = _split3(dsel)
            dmat = _dot_tn(dh, ones) + _dot_tn(dm, ones) + _dot_tn(dl, ones)
            xsel = jnp.where(pick, xdt[:, cols], 0.0)
            xh = xsel.astype(BF16)
            xl = (xsel - xh.astype(F32)).astype(BF16)
            bg = bm[:, nsl].astype(BF16)
            new = dmat * st_ref[s, cols, :] + _dot_tn(xh, bg) + _dot_tn(xl, bg)
            st_out_ref[s, cols, :] = new
            csel = jnp.where(pick, cm[:, nsl], 0.0).astype(BF16)
            y_parts[g].append(_dot_nt(csel, new.astype(BF16)))
    halves = []
    for g in range(SSM_GROUPS):
        acc = y_parts[g][0]
        for part in y_parts[g][1:]:
            acc = acc + part
        halves.append(acc)
    y = jnp.concatenate(halves, axis=1) + dexp_ref[...] * xs
    o_ref[...] = _ssd_gate_norm(y, z_ref[...], nrm_ref[...])


def _ssd_step(h, cst, st, cw, cb, dtb, alog, dexp, nrm, expand):
    ns = h.shape[0]
    g8 = SUBLANES

    def col(width, off):
        return pl.BlockSpec((g8, width), lambda i: (i, off // width))

    def full(shape):
        return pl.BlockSpec(shape, lambda i: (0, 0))

    cst_spec = pl.BlockSpec((CONV_WIDTH - 1, g8, CONV_DIM), lambda i: (0, i, 0))
    st_spec = pl.BlockSpec((g8, SSM_INNER, SSM_STATE), lambda i: (i, 0, 0))
    return pl.pallas_call(
        _ssd_step_kernel,
        grid=(ns // g8,),
        in_specs=[col(CONV_DIM, OFF_XBC), col(SSM_INNER, OFF_Z), col(LANES, OFF_DT), cst_spec, st_spec,
                  full((CONV_WIDTH, CONV_DIM)), full((1, CONV_DIM)), full((1, LANES)), full((1, LANES)),
                  full((1, SSM_INNER)), full((1, SSM_INNER)), full((LANES, SSM_INNER))],
        out_specs=[pl.BlockSpec((g8, SSM_INNER), lambda i: (i, 0)), cst_spec, st_spec],
        out_shape=[jax.ShapeDtypeStruct((ns, SSM_INNER), F32),
                   jax.ShapeDtypeStruct(cst.shape, F32),
                   jax.ShapeDtypeStruct(st.shape, F32)],
        compiler_params=_cparams("parallel"),
        name="ssd_step",
    )(h, h, h, cst, st, cw, cb, dtb, alog, dexp, nrm, expand)


PAGES_PER_CHUNK = 16
SAMPLE_KC = PAGES_PER_CHUNK * PAGE_SIZE
N_SLOTS = 256


def _selected_positions(sel2, n_slots):
    n_p = sel2.shape[0]
    sel_b = sel2.astype(BF16)
    oi = lax.broadcasted_iota(I32, (PAGE_SIZE, PAGE_SIZE), 0)
    oj = lax.broadcasted_iota(I32, (PAGE_SIZE, PAGE_SIZE), 1)
    inrow = _dot(sel_b, jnp.where(oi <= oj, 1.0, 0.0).astype(BF16))
    rowtot = _dot(sel_b, jnp.ones((PAGE_SIZE, PAGE_SIZE), BF16))
    pi = lax.broadcasted_iota(I32, (n_p, n_p), 0)
    pj = lax.broadcasted_iota(I32, (n_p, n_p), 1)
    rowpre = _dot(jnp.where(pj < pi, 1.0, 0.0).astype(BF16), rowtot.astype(BF16))
    cum = rowpre + inrow
    rowcum = (rowpre + rowtot)[:, 0:1]
    slot = lax.broadcasted_iota(I32, (1, n_slots), 1).astype(F32)
    page_s = jnp.sum(jnp.where(rowcum <= slot, 1.0, 0.0), axis=0, keepdims=True)
    prow = lax.broadcasted_iota(I32, (n_p, n_slots), 0).astype(F32)
    onehot_t = jnp.where(prow == page_s, 1.0, 0.0).astype(BF16)
    cum_of_page = _dot(cum.T.astype(BF16), onehot_t)
    off_s = jnp.sum(jnp.where(cum_of_page <= slot, 1.0, 0.0), axis=0, keepdims=True)
    return page_s * PAGE_SIZE + off_s


def _dsa_sample_kernel(pt_ref, q_ref, iqh_ref, iql_ref, ikw_ref, kn_ref, vn_ref, ikn_ref,
                       ck_hbm, cv_hbm, cik_hbm, o_ref,
                       ikbuf, kg, vg, key_ref, pos_vmem, pos_smem, sem_ik, sem_kv, sem_pos,
                       *, n_pages, n_pool, layer, k_sel, idx_bits):
    b = pl.program_id(0)
    kc = SAMPLE_KC
    n_chunks = n_pages // PAGES_PER_CHUNK
    past = n_pages * PAGE_SIZE

    def ik_copy(p):
        return pltpu.make_async_copy(cik_hbm.at[layer * n_pool + pt_ref[b, p]], ikbuf.at[p], sem_ik)

    def start_ik(p, _):
        ik_copy(p).start()
        return 0

    def wait_ik(p, _):
        ik_copy(p).wait()
        return 0

    lax.fori_loop(0, n_pages, start_ik, 0)
    lax.fori_loop(0, n_pages, wait_ik, 0)

    def stack(parts):
        zero = jnp.zeros((SUBLANES - IDX_HEADS, IDX_DIM), F32)
        cols = [jnp.concatenate([p[:, h * IDX_DIM:(h + 1) * IDX_DIM].astype(F32) for h in range(IDX_HEADS)]
                                + [zero], axis=0) for p in parts]
        return jnp.concatenate(cols + [jnp.zeros((SUBLANES, IDX_DIM), F32)], axis=1).astype(BF16)

    iqh = iqh_ref[...]
    iql = iql_ref[...]
    lhs = stack([iqh, iqh, iql])
    iw = ikw_ref[...]
    iw_col = jnp.concatenate([iw[:, IDX_DIM + h:IDX_DIM + h + 1] for h in range(IDX_HEADS)]
                             + [jnp.zeros((SUBLANES - IDX_HEADS, 1), F32)], axis=0)

    def score_of(ik):
        x = jnp.concatenate([ik, jnp.zeros_like(ik)], axis=1)
        kh = x.astype(BF16)
        kl = pltpu.roll(x - kh.astype(F32), IDX_DIM, 1).astype(BF16)
        d = jnp.maximum(_dot_nt(lhs, jnp.concatenate([kh + kl, kh], axis=1)), 0.0)
        return jnp.sum(iw_col * d, axis=0, keepdims=True) * IDX_SCALE

    for c in range(n_chunks):
        sc = score_of(ikbuf[c * PAGES_PER_CHUNK:(c + 1) * PAGES_PER_CHUNK].reshape(kc, IDX_DIM))
        key_c = _sort_key(sc)
        for i in range(PAGES_PER_CHUNK):
            p = c * PAGES_PER_CHUNK + i
            key_ref[p:p + 1, :] = key_c[:, i * PAGE_SIZE:(i + 1) * PAGE_SIZE]
    ikn = ikn_ref[...]
    s_new = score_of(jnp.concatenate([ikn, jnp.zeros((SUBLANES - 1, IDX_DIM), F32)], axis=0))[:, 0:1]
    tail = (lax.broadcasted_iota(I32, (SUBLANES, LANES), 0) == 0) & (
        lax.broadcasted_iota(I32, (SUBLANES, LANES), 1) == 0)
    key_ref[n_pages:n_pages + SUBLANES, :] = _sort_key(jnp.where(tail, s_new, -jnp.inf))

    member = _topk_select_single(key_ref[...], k_sel, idx_bits)
    sel_all = jnp.where(member, 1.0, 0.0)
    sel2 = sel_all[0:n_pages, :]
    sel_new = sel_all[n_pages:n_pages + 1, 0:1] > 0.5
    n_valid = jnp.sum(sel2).astype(I32)
    pos_vmem[...] = jnp.broadcast_to(_selected_positions(sel2, N_SLOTS).astype(I32), pos_vmem.shape)
    to_smem = pltpu.make_async_copy(pos_vmem, pos_smem, sem_pos)
    to_smem.start()
    to_smem.wait()

    def row_copies(s):
        pos = pos_smem[0, s]
        pos = jnp.where(pos < past, pos, 0)
        page = pt_ref[b, lax.shift_right_logical(pos, PAGE_SIZE.bit_length() - 1)]
        off = pos & (PAGE_SIZE - 1)
        return (pltpu.make_async_copy(ck_hbm.at[layer, page, off], kg.at[s], sem_kv.at[0]),
                pltpu.make_async_copy(cv_hbm.at[layer, page, off], vg.at[s], sem_kv.at[1]))

    def start_rows(s, _):
        ck, cv = row_copies(s)
        ck.start()
        cv.start()
        return 0

    def wait_rows(s, _):
        ck, cv = row_copies(s)
        ck.wait()
        cv.wait()
        return 0

    lax.fori_loop(0, N_SLOTS, start_rows, 0)
    lax.fori_loop(0, N_SLOTS, wait_rows, 0)

    kgv = kg[...]
    vgv = vg[...]
    kn = kn_ref[...]
    vn = vn_ref[...]
    valid = lax.broadcasted_iota(I32, (N_SLOTS, 1, 1), 0) < n_valid
    for r in range(N_HEADS // N_KV_HEADS):
        qr = q_ref[r]
        s = jnp.where(valid, jnp.sum(kgv * qr[None], axis=2, keepdims=True), NEG_BIG)
        s_n = jnp.where(sel_new, jnp.sum(kn * qr, axis=1, keepdims=True), NEG_BIG)
        m = jnp.maximum(jnp.max(s, axis=0), s_n)
        p = jnp.exp2(s - m[None])
        p_n = jnp.exp2(s_n - m)
        denom = jnp.sum(p, axis=0) + p_n
        o_ref[r] = (jnp.sum(p * vgv, axis=0) + p_n * vn) / denom


def _dsa_sample(page_table, qb, iqh, iql, h, ko, vo, iko, cache_k, cache_v, cache_ik, layer):
    ns, n_pages = page_table.shape
    depth, n_pool = cache_k.shape[0], cache_k.shape[1]
    total = n_pages * PAGE_SIZE + 1
    k_sel = min(TOPK_MAX, total // 4)
    assert k_sel <= N_SLOTS and n_pages % PAGES_PER_CHUNK == 0
    idx_bits = max(1, (total - 1).bit_length())
    n_chunks = n_pages // PAGES_PER_CHUNK
    rep = N_HEADS // N_KV_HEADS

    def row(width):
        return pl.BlockSpec((None, 1, width), lambda b, pt: (b, 0, 0))

    def heads(lead):
        return pl.BlockSpec((None,) + lead + (N_KV_HEADS, HEAD_DIM), lambda b, pt: (b,) + (0,) * (len(lead) + 2))

    any_spec = pl.BlockSpec(memory_space=pl.ANY)
    ikw = h[:, OFF_IKW:OFF_IKW + LANES]
    q4 = qb.astype(F32).reshape(ns, N_KV_HEADS, rep, HEAD_DIM).transpose(0, 2, 1, 3)
    grid_spec = pltpu.PrefetchScalarGridSpec(
        num_scalar_prefetch=1,
        grid=(ns,),
        in_specs=[heads((rep,)), row(KV_WIDTH), row(KV_WIDTH), row(LANES), heads(()), heads(()),
                  row(IDX_DIM), any_spec, any_spec, any_spec],
        out_specs=heads((rep,)),
        scratch_shapes=[
            pltpu.VMEM((n_pages, PAGE_SIZE, IDX_DIM), F32),
            pltpu.VMEM((N_SLOTS, N_KV_HEADS, HEAD_DIM), F32),
            pltpu.VMEM((N_SLOTS, N_KV_HEADS, HEAD_DIM), F32),
            pltpu.VMEM((n_pages + SUBLANES, PAGE_SIZE), I32),
            pltpu.VMEM((SUBLANES, N_SLOTS), I32),
            pltpu.SMEM((SUBLANES, N_SLOTS), I32),
            pltpu.SemaphoreType.DMA(()),
            pltpu.SemaphoreType.DMA((2,)),
            pltpu.SemaphoreType.DMA(()),
        ],
    )
    r3 = lambda a: a.reshape(ns, 1, a.shape[-1])
    out = pl.pallas_call(
        functools.partial(_dsa_sample_kernel, n_pages=n_pages, n_pool=n_pool, layer=layer, k_sel=k_sel,
                          idx_bits=idx_bits),
        grid_spec=grid_spec,
        out_shape=jax.ShapeDtypeStruct((ns, rep, N_KV_HEADS, HEAD_DIM), F32),
        compiler_params=_cparams("arbitrary"),
        name="dsa_sample",
    )(page_table, q4, r3(iqh), r3(iql), r3(ikw), ko, vo, r3(iko), cache_k, cache_v,
      cache_ik.reshape(depth * n_pool, PAGE_SIZE, IDX_DIM))
    return out.reshape(ns, ATT_WIDTH)


def _complex_pow(re, im, n):
    rr, ri = jnp.ones_like(re), jnp.zeros_like(im)
    while n:
        if n & 1:
            rr, ri = rr * re - ri * im, rr * im + ri * re
        re, im = re * re - im * im, 2.0 * re * im
        n >>= 1
    return rr, ri


def _layer_params(lw, lseg):
    w = lw["w_in"]
    offs = [0]
    for s in IN_SIZES:
        offs.append(offs[-1] + s)
    u_, q_, k_, v_, iq_, ik_, iw_, z_, xbc_, dt_, gt_ = [w[:, offs[i]:offs[i + 1]] for i in range(len(IN_SIZES))]
    zpad = lambda n: jnp.zeros((D_MODEL, n), w.dtype)
    wp = jnp.concatenate([gt_, z_, u_, xbc_, q_, k_, v_, iq_, ik_, iw_, zpad(LANES - IDX_DIM - IDX_HEADS),
                          dt_, zpad(LANES - SSM_HEADS)], axis=1).astype(BF16)
    p = {"wp": wp, "norm_mix": lw["norm_mix"].reshape(1, D_MODEL)}
    p["qg"] = jnp.tile(lw["q_norm"], N_HEADS).reshape(1, ATT_WIDTH)
    p["kg"] = jnp.tile(lw["k_norm"], N_KV_HEADS).reshape(1, KV_WIDTH)
    ones = jnp.ones((HEAD_DIM, HEAD_DIM), BF16)
    p["bdq"] = jnp.kron(jnp.eye(N_HEADS, dtype=BF16), ones)
    p["bdk"] = jnp.kron(jnp.eye(N_KV_HEADS, dtype=BF16), ones)

    ar, ai = lw["s5_a_re"], lw["s5_a_im"]
    dt = jnp.exp(lw["s5_log_dt"])[:, None]
    mag = jnp.exp(dt * ar)
    abar_re, abar_im = mag * jnp.cos(dt * ai), mag * jnp.sin(dt * ai)
    den = ar * ar + ai * ai
    nr, ni = abar_re - 1.0, abar_im
    coef_re = (nr * ar + ni * ai) / den
    coef_im = (ni * ar - nr * ai) / den
    bc_re = coef_re[:, :, None] * lw["s5_b_re"] - coef_im[:, :, None] * lw["s5_b_im"]
    bc_im = coef_re[:, :, None] * lw["s5_b_im"] + coef_im[:, :, None] * lw["s5_b_re"]
    eye = jnp.eye(S5_GROUPS, dtype=F32)
    embed_b = lambda b: jnp.einsum("gnc,gh->gchn", b, eye).reshape(S5_WIDTH, S5_LANES)
    embed_c = lambda c: jnp.einsum("gcn,gh->gnhc", c, eye).reshape(S5_LANES, S5_WIDTH)
    p["bmat"] = jnp.concatenate([embed_b(bc_re), embed_b(bc_im)], axis=1).astype(BF16)
    p["cmat"] = jnp.concatenate([embed_c(lw["s5_c_re"]), -embed_c(lw["s5_c_im"])], axis=0).astype(BF16)
    p["ar"] = abar_re.reshape(1, S5_LANES)
    p["ai"] = abar_im.reshape(1, S5_LANES)
    pr, pi = _complex_pow(abar_re, abar_im, lseg)
    p["pr"] = pr.reshape(1, S5_LANES)
    p["pi"] = pi.reshape(1, S5_LANES)
    p["dskip"] = lw["s5_d"].reshape(1, S5_WIDTH)
    p["wglu"] = lw["s5_w_glu"].astype(BF16)

    p["cw"] = lw["conv_w"]
    p["cb"] = lw["conv_b"].reshape(1, CONV_DIM)
    padh = lambda v: jnp.pad(v, (0, LANES - SSM_HEADS)).reshape(1, LANES)
    p["dtb"] = padh(lw["dt_bias"])
    p["alog"] = padh(lw["a_log"])
    p["dexp"] = jnp.repeat(lw["ssm_d"], SSM_HEAD_DIM).reshape(1, SSM_INNER)
    p["nrm"] = lw["ssm_norm"].reshape(1, SSM_INNER)
    p["expand"] = jnp.pad(jnp.kron(jnp.eye(SSM_HEADS, dtype=F32), jnp.ones((1, SSM_HEAD_DIM), F32)),
                          ((0, LANES - SSM_HEADS), (0, 0))).astype(BF16)

    p["w1"] = lw["w_br_s5"].astype(BF16)
    p["w2"] = (lw["w_br_att"].reshape(N_KV_HEADS, 2, HEAD_DIM, D_MODEL).transpose(1, 0, 2, 3)
               .reshape(ATT_WIDTH, D_MODEL).astype(BF16))
    p["w3"] = lw["w_br_ssm"].astype(BF16)
    p["wo"] = lw["w_o"].astype(BF16)
    p["norm_mlp"] = lw["norm_mlp"].reshape(1, D_MODEL)
    p["wu"] = lw["w_up"].astype(BF16)
    p["wd"] = lw["w_down"].astype(BF16)
    return p


def _rope_table(pos):
    half = ROT_DIM // 2
    inv_freq = ROPE_THETA ** (-jnp.arange(half, dtype=F32) / half)
    ang = pos.astype(F32)[:, None] * inv_freq[None, :]
    cos, sin = jnp.cos(ang), jnp.sin(ang)
    n = pos.shape[0]
    rest = HEAD_DIM - ROT_DIM
    c = jnp.concatenate([cos, cos, jnp.ones((n, rest), F32)], axis=1)
    sa = jnp.concatenate([-sin, jnp.zeros((n, half + rest), F32)], axis=1)
    sb = jnp.concatenate([jnp.zeros((n, half), F32), sin, jnp.zeros((n, rest), F32)], axis=1)
    rep = LANES // HEAD_DIM
    return jnp.concatenate([jnp.tile(c, (1, rep)), jnp.tile(sa, (1, rep)), jnp.tile(sb, (1, rep))], axis=1)


def _pick(n, pref):
    for t in pref:
        if n % t == 0:
            return t
    return n


def _layer_prompt(x2, p, tab, nb, lt):
    m = nb * lt
    lseg = lt // SUBLANES
    h = _inproj(x2, p["norm_mix"], p["wp"], _pick(m, (1024, 512, 256, 128)))
    kc = _pick(lt, (512, 256, 128))
    qb, ko, kb, vo, vt, iqh, iql, iko, ikc = _post(
        h, tab, p["qg"], p["kg"], p["bdq"], p["bdk"], nb, lt, kc, True)
    oatt = _dsa_prompt(qb, iqh, iql, h, kb, vt, ikc, nb, lt, kc)

    u4 = h.reshape(nb, SUBLANES, lseg, NP)
    tt = _pick(lseg, (64, 32, 16, 8))
    ends = _s5_ends(u4, OFF_U // S5_WIDTH, p["bmat"], p["ar"], p["ai"], tt)
    h0 = _s5_carry(ends, p["pr"], p["pi"])
    os5, hl = _s5_main(u4, OFF_U // S5_WIDTH, h0, p["bmat"], p["cmat"], p["ar"], p["ai"], p["dskip"],
                       p["wglu"], tt)

    ossm, st = _ssd_prompt(h, p["cw"], p["cb"], p["dtb"], p["alog"], p["dexp"], p["nrm"], p["expand"], nb, lt)

    tm = _pick(m, (512, 256, 128))
    x1 = _merge(x2, h, os5.reshape(m, S5_WIDTH), oatt, ossm, p["w1"], p["w2"], p["w3"], p["wo"], tm)
    xo = _mlp(x1, p["norm_mlp"], p["wu"], p["wd"], _pick(m, (256, 128)))

    new_conv = h.reshape(nb, lt, NP)[:, lt - (CONV_WIDTH - 1):, OFF_XBC:OFF_XBC + CONV_DIM]
    state = (ko.reshape(nb, lt, N_KV_HEADS, HEAD_DIM), vo.reshape(nb, lt, N_KV_HEADS, HEAD_DIM),
             iko.reshape(nb, lt, IDX_DIM),
             hl[:, SUBLANES - 1, 0:S5_LANES].reshape(nb, S5_GROUPS, S5_STATE),
             hl[:, SUBLANES - 1, S5_LANES:].reshape(nb, S5_GROUPS, S5_STATE),
             new_conv, st.reshape(nb, SSM_HEADS, SSM_HEAD_DIM, SSM_STATE))
    return xo, state


def _layer_sample(x2, p, tab, caches, states):
    cache_k, cache_v, cache_ik, page_table, layer = caches
    s5_re, s5_im, conv0, ssm0 = states
    ns = x2.shape[0]
    ng = ns // SUBLANES
    h = _inproj(x2, p["norm_mix"], p["wp"], ns)
    qb, ko, kb, vo, vb, iqh, iql, iko, _ = _post(
        h, tab, p["qg"], p["kg"], p["bdq"], p["bdk"], 1, ns, ns, False)
    oatt = _dsa_sample(page_table, qb, iqh, iql, h, ko, vo, iko, cache_k, cache_v, cache_ik, layer)

    u4 = h.reshape(ng, SUBLANES, 1, NP)
    h0 = jnp.concatenate([s5_re.reshape(ng, SUBLANES, S5_LANES), s5_im.reshape(ng, SUBLANES, S5_LANES)], axis=2)
    os5, hl = _s5_main(u4, OFF_U // S5_WIDTH, h0, p["bmat"], p["cmat"], p["ar"], p["ai"], p["dskip"],
                       p["wglu"], 1)

    ossm, cst, st = _ssd_step(h, conv0.transpose(1, 0, 2), ssm0.reshape(ns, SSM_INNER, SSM_STATE),
                              p["cw"], p["cb"], p["dtb"], p["alog"], p["dexp"], p["nrm"], p["expand"])

    x1 = _merge(x2, h, os5.reshape(ns, S5_WIDTH), oatt, ossm, p["w1"], p["w2"], p["w3"], p["wo"], ns)
    xo = _mlp(x1, p["norm_mlp"], p["wu"], p["wd"], ns)
    state = (ko.reshape(ns, 1, N_KV_HEADS, HEAD_DIM), vo.reshape(ns, 1, N_KV_HEADS, HEAD_DIM),
             iko.reshape(ns, 1, IDX_DIM),
             hl[:, :, 0:S5_LANES].reshape(ns, S5_GROUPS, S5_STATE),
             hl[:, :, S5_LANES:].reshape(ns, S5_GROUPS, S5_STATE),
             cst.transpose(1, 0, 2), st.reshape(ns, SSM_HEADS, SSM_HEAD_DIM, SSM_STATE))
    return xo, state


_WEIGHTS =("norm_mix", "w_in", "q_norm", "k_norm", "s5_a_re", "s5_a_im", "s5_log_dt", "s5_b_re", "s5_b_im",
            "s5_c_re", "s5_c_im", "s5_d", "s5_w_glu", "conv_w", "conv_b", "dt_bias", "a_log", "ssm_d", "ssm_norm",
            "w_br_s5", "w_br_att", "w_br_ssm", "w_o", "norm_mlp", "w_up", "w_down")


def kernel(x_prompt, x_sample, cache_k, cache_v, cache_idx_k, state_s5_re, state_s5_im, state_conv, state_ssm, page_table, norm_mix, w_in, q_norm, k_norm, s5_a_re, s5_a_im, s5_log_dt, s5_b_re, s5_b_im, s5_c_re, s5_c_im, s5_d, s5_w_glu, conv_w, conv_b, dt_bias, a_log, ssm_d, ssm_norm, w_br_s5, w_br_att, w_br_ssm, w_o, norm_mlp, w_up, w_down):
    weights = dict(zip(_WEIGHTS, (norm_mix, w_in, q_norm, k_norm, s5_a_re, s5_a_im, s5_log_dt, s5_b_re, s5_b_im,
                                  s5_c_re, s5_c_im, s5_d, s5_w_glu, conv_w, conv_b, dt_bias, a_log, ssm_d, ssm_norm,
                                  w_br_s5, w_br_att, w_br_ssm, w_o, norm_mlp, w_up, w_down)))
    depth = w_in.shape[0]
    nb, lt, _ = x_prompt.shape
    ns, n_new, _ = x_sample.shape
    assert n_new == 1 and ns % SUBLANES == 0 and lt % (SUBLANES * Q_BLOCK) == 0
    past = page_table.shape[1] * PAGE_SIZE
    tab_p = _rope_table(jnp.arange(lt))
    tab_s = _rope_table(jnp.full((ns,), past, jnp.int32))
    xp = x_prompt.reshape(nb * lt, D_MODEL)
    xs = x_sample.reshape(ns, D_MODEL)
    st_p, st_s = [], []
    for l in range(depth):
        p = _layer_params({k: v[l] for k, v in weights.items()}, lt // SUBLANES)
        xp, sp = _layer_prompt(xp, p, tab_p, nb, lt)
        caches = (cache_k, cache_v, cache_idx_k, page_table, l)
        states = (state_s5_re[l], state_s5_im[l], state_conv[l], state_ssm[l])
        xs, ss = _layer_sample(xs, p, tab_s, caches, states)
        st_p.append(sp)
        st_s.append(ss)
    outs_p = [jnp.stack([s[i] for s in st_p]) for i in range(7)]
    outs_s = [jnp.stack([s[i] for s in st_s]) for i in range(7)]
    return (xp.reshape(nb, lt, D_MODEL), xs.reshape(ns, 1, D_MODEL), *outs_p, *outs_s)
```

```python
import functools
import math

import jax
import jax.numpy as jnp
from jax import lax
from jax.experimental import pallas as pl
from jax.experimental.pallas import tpu as pltpu

F32 = jnp.float32
BF16 = jnp.bfloat16
I32 = jnp.int32

D_MODEL = 1024
EPS = 1e-6
S5_GROUP = 16
S5_WIDTH = 512
S5_GROUPS = 32
S5_STATE = 64
S5_LANES = S5_GROUPS * S5_STATE
N_HEADS = 8
N_KV_HEADS = 4
HEAD_DIM = 64
ATT_WIDTH = 512
KV_WIDTH = 256
ROT_DIM = 16
ROPE_THETA = 500000.0
IDX_HEADS = 4
IDX_DIM = 64
IDX_SCALE = (IDX_HEADS * IDX_DIM) ** -0.5
TOPK_MAX = 256
Q_BLOCK = 128
PAGE_SIZE = 128
SSM_INNER = 1024
SSM_HEAD_DIM = 64
SSM_HEADS = 16
SSM_GROUPS = 2
SSM_STATE = 128
CONV_WIDTH = 4
CONV_DIM = 1536
SSD_CHUNK = 128
D_FF = 4096
IN_SIZES = (512, 512, 256, 256, 256, 64, 4, 1024, 1536, 16, 3072)

LANES = 128
SUBLANES = 8
VMEM_LIMIT = 56 * 1024 * 1024

NP = 7680
OFF_GATES, OFF_Z, OFF_U, OFF_XBC, OFF_Q, OFF_K, OFF_V, OFF_IQ, OFF_IKW, OFF_DT = (
    0, 3072, 4096, 4608, 6144, 6656, 6912, 7168, 7424, 7552)

NEG_BIG = -1e30
Q_SCALE = HEAD_DIM ** -0.5 * math.log2(math.e)
V_ROWS = HEAD_DIM + 16
INT_MIN = -(2 ** 31)


def _cparams(*sem):
    return pltpu.CompilerParams(dimension_semantics=sem, vmem_limit_bytes=VMEM_LIMIT)


def _split3(x):
    hi = x.astype(BF16)
    r = x - hi.astype(F32)
    mid = r.astype(BF16)
    lo = (r - mid.astype(F32)).astype(BF16)
    return hi, mid, lo


def _dot(a, b):
    return jnp.dot(a, b, preferred_element_type=F32)


def _dot_nt(a, b):
    return lax.dot_general(a, b, (((1,), (1,)), ((), ())), preferred_element_type=F32)


def _dot_tn(a, b):
    return lax.dot_general(a, b, (((0,), (0,)), ((), ())), preferred_element_type=F32)


def _dot3_exact_rhs(x, m_bf16):
    hi, mid, lo = _split3(x)
    return _dot(hi, m_bf16) + _dot(mid, m_bf16) + _dot(lo, m_bf16)


def _dot3_exact_lhs(m_bf16, x):
    hi, mid, lo = _split3(x)
    return _dot(m_bf16, hi) + _dot(m_bf16, mid) + _dot(m_bf16, lo)


def _inproj_kernel(x_ref, g_ref, w_ref, o_ref, xn_ref):
    @pl.when(pl.program_id(1) == 0)
    def _():
        x = x_ref[...]
        ms = jnp.mean(x * x, axis=-1, keepdims=True)
        xn_ref[...] = (x * lax.rsqrt(ms + EPS) * g_ref[...]).astype(BF16)

    o_ref[...] = _dot(xn_ref[...], w_ref[...])


def _inproj(x, g, wp, tm, tn=1536):
    m = x.shape[0]
    return pl.pallas_call(
        _inproj_kernel,
        grid=(m // tm, NP // tn),
        in_specs=[
            pl.BlockSpec((tm, D_MODEL), lambda i, j: (i, 0)),
            pl.BlockSpec((1, D_MODEL), lambda i, j: (0, 0)),
            pl.BlockSpec((D_MODEL, tn), lambda i, j: (0, j)),
        ],
        out_specs=pl.BlockSpec((tm, tn), lambda i, j: (i, j)),
        out_shape=jax.ShapeDtypeStruct((m, NP), F32),
        scratch_shapes=[pltpu.VMEM((tm, D_MODEL), BF16)],
        compiler_params=_cparams("parallel", "arbitrary"),
        name="inproj",
    )(x, g, wp)


def _rope(x, c, sa, sb):
    outs = []
    for t in range(x.shape[1] // LANES):
        xt = x[:, t * LANES:(t + 1) * LANES]
        outs.append(xt * c + pltpu.roll(xt, LANES - ROT_DIM // 2, 1) * sa + pltpu.roll(xt, ROT_DIM // 2, 1) * sb)
    return outs[0] if len(outs) == 1 else jnp.concatenate(outs, axis=1)


def _head_rms(x, bd_ref, gain):
    ms = _dot3_exact_rhs(x * x, bd_ref[...]) * (1.0 / HEAD_DIM)
    return x * lax.rsqrt(ms + EPS) * gain


def _post_kernel(q_ref, k_ref, v_ref, iq_ref, ikw_ref, tab_ref, qg_ref, kg_ref, bdq_ref, bdk_ref,
                 qb_ref, ko_ref, kb_ref, vo_ref, vb_ref, iqh_ref, iql_ref, iko_ref, ikc_ref,
                 *, v_transposed):
    c = tab_ref[:, 0:LANES]
    sa = tab_ref[:, LANES:2 * LANES]
    sb = tab_ref[:, 2 * LANES:3 * LANES]
    q = _rope(_head_rms(q_ref[...], bdq_ref, qg_ref[...]), c, sa, sb)
    qb_ref[...] = (q * Q_SCALE).astype(BF16)
    k = _rope(_head_rms(k_ref[...], bdk_ref, kg_ref[...]), c, sa, sb)
    ko_ref[...] = k
    kb_ref[...] = k.astype(BF16)
    v = v_ref[...]
    vo_ref[...] = v
    if v_transposed:
        v_t = v.T.astype(BF16)
        ones = jnp.ones((V_ROWS - HEAD_DIM, v.shape[0]), BF16)
        vb_ref[...] = jnp.concatenate(
            [blk for g in range(N_KV_HEADS) for blk in (v_t[g * HEAD_DIM:(g + 1) * HEAD_DIM, :], ones)], axis=0)
    else:
        vb_ref[...] = v.astype(BF16)
    iq = _rope(iq_ref[...], c, sa, sb)
    iqh = iq.astype(BF16)
    iqh_ref[...] = iqh
    iql_ref[...] = (iq - iqh.astype(F32)).astype(BF16)
    ikr = _rope(ikw_ref[...], c, sa, sb)
    iko_ref[...] = ikr[:, 0:IDX_DIM]
    lane = lax.broadcasted_iota(I32, ikr.shape, 1)
    ik = jnp.where(lane < IDX_DIM, ikr, 0.0)
    ikh = ik.astype(BF16)
    ikl = pltpu.roll(ik - ikh.astype(F32), IDX_DIM, 1).astype(BF16)
    ikc_ref[...] = jnp.concatenate([ikh + ikl, ikh], axis=1)


def _post(h, tab, qg, kg, bdq, bdk, nb, lt, tt, v_transposed):
    m = nb * lt
    nt = lt // tt

    def col(width, off):
        return pl.BlockSpec((tt, width), lambda b, i: (b * nt + i, off // width))

    def out(width):
        return pl.BlockSpec((tt, width), lambda b, i: (b * nt + i, 0))

    def full(shape):
        return pl.BlockSpec(shape, lambda b, i: (0, 0))

    shapes = [
        (ATT_WIDTH, BF16), (KV_WIDTH, F32), (KV_WIDTH, BF16), (KV_WIDTH, F32), (KV_WIDTH, BF16),
        (KV_WIDTH, BF16), (KV_WIDTH, BF16), (IDX_DIM, F32), (KV_WIDTH, BF16)]
    out_specs = [out(w) for w, _ in shapes]
    out_shape = [jax.ShapeDtypeStruct((m, w), d) for w, d in shapes]
    if v_transposed:
        out_specs[4] = pl.BlockSpec((None, N_KV_HEADS * V_ROWS, tt), lambda b, i: (b * nt + i, 0, 0))
        out_shape[4] = jax.ShapeDtypeStruct((m // tt, N_KV_HEADS * V_ROWS, tt), BF16)
    return pl.pallas_call(
        functools.partial(_post_kernel, v_transposed=v_transposed),
        grid=(nb, nt),
        in_specs=[
            col(ATT_WIDTH, OFF_Q), col(KV_WIDTH, OFF_K), col(KV_WIDTH, OFF_V), col(KV_WIDTH, OFF_IQ),
            col(LANES, OFF_IKW),
            pl.BlockSpec((tt, 3 * LANES), lambda b, i: (i, 0)),
            full((1, ATT_WIDTH)), full((1, KV_WIDTH)), full((ATT_WIDTH, ATT_WIDTH)), full((KV_WIDTH, KV_WIDTH)),
        ],
        out_specs=out_specs,
        out_shape=out_shape,
        compiler_params=_cparams("parallel", "parallel"),
        name="qk_post",
    )(h, h, h, h, h, tab, qg, kg, bdq, bdk)


def _sort_key(score):
    score = jnp.where(score == 0.0, 0.0, score)
    bits = pltpu.bitcast(score, I32)
    return bits ^ (lax.shift_right_arithmetic(bits, 31) & 0x7FFFFFFF)


def _topk_select_single(keys, k_sel, idx_bits):
    def count(hit):
        return jnp.sum(jnp.sum(hit.astype(I32), axis=0, keepdims=True), axis=1, keepdims=True)

    def bit_body(i, carry):
        t_u, cnt_t = carry
        cand_u = t_u | lax.shift_left(jnp.int32(1), 31 - i)
        cnt = count(keys >= (cand_u ^ INT_MIN))
        take = cnt >= k_sel
        return jnp.where(take, cand_u, t_u), jnp.where(take, cnt, cnt_t)

    init = (jnp.zeros((1, 1), I32), jnp.full((1, 1), keys.shape[0] * LANES, I32))
    t_u, cnt_t = lax.fori_loop(0, 32, bit_body, init)
    t = t_u ^ INT_MIN
    pos = (lax.broadcasted_iota(I32, keys.shape, 0) * LANES + lax.broadcasted_iota(I32, keys.shape, 1))

    def tie_search(_):
        need_m1 = k_sel - 1 - count(keys > t)

        def jbody(i, j):
            cand = j | lax.shift_left(jnp.int32(1), idx_bits - 1 - i)
            f = count((keys == t) & (pos < cand))
            return jnp.where(f <= need_m1, cand, j)

        return lax.fori_loop(0, idx_bits, jbody, jnp.zeros((1, 1), I32))

    jcut = lax.cond(jnp.max(cnt_t) > k_sel, tie_search, lambda _: jnp.full((1, 1), 2 ** 30, I32), 0)
    return (keys > t) | ((keys == t) & (pos <= jcut))


NEG_INF_KEY = -2139095041


def _topk_threshold_t(key_ref, nkc, kc, k_sel, idx_bits):
    def count(pred):
        def body(c, acc):
            hit = pred(key_ref[c], c).astype(I32)
            return acc + jnp.sum(hit.reshape(kc // SUBLANES, SUBLANES, LANES), axis=0)
        acc = lax.fori_loop(0, nkc, body, jnp.zeros((SUBLANES, LANES), I32))
        return jnp.sum(acc, axis=0, keepdims=True)

    def bit_body(i, carry):
        t_u, cnt_t = carry
        cand_u = t_u | lax.shift_left(jnp.int32(1), 31 - i)
        cand_s = cand_u ^ INT_MIN
        cnt = count(lambda kk, c: kk >= cand_s)
        take = cnt >= k_sel
        return jnp.where(take, cand_u, t_u), jnp.where(take, cnt, cnt_t)

    t_u0 = jnp.zeros((1, LANES), I32)
    cnt0 = jnp.full((1, LANES), 1, I32) * (nkc * kc)
    t_u, cnt_t = lax.fori_loop(0, 32, bit_body, (t_u0, cnt0))
    t = t_u ^ INT_MIN

    @pl.when(jnp.max(cnt_t) > k_sel)
    def _():
        need_m1 = k_sel - 1 - count(lambda kk, c: kk > t)

        def pos_of(c):
            return c * kc + lax.broadcasted_iota(I32, (kc, LANES), 0)

        def jbody(i, j):
            cand = j | lax.shift_left(jnp.int32(1), idx_bits - 1 - i)
            f = count(lambda kk, c: (kk == t) & (pos_of(c) < cand))
            return jnp.where(f <= need_m1, cand, j)

        jcut = lax.fori_loop(0, idx_bits, jbody, jnp.zeros((1, LANES), I32))

        def demote(c, _):
            kk = key_ref[c]
            key_ref[c] = jnp.where((kk == t) & (pos_of(c) > jcut), t - 1, kk)
            return 0

        lax.fori_loop(0, nkc, demote, 0)

    return jnp.maximum(t, NEG_INF_KEY + 1)


def _dsa_kernel(qb_ref, iqh_ref, iql_ref, ikw_ref, kb_ref, vt_ref, ikc_ref, o_ref,
                key_ref, wq_ref, wi_ref, s_ref, bias_ref, m_ref, acc_ref, *, kc, k_sel, idx_bits):
    qi = pl.program_id(1)
    nkc = (qi * Q_BLOCK) // kc + 1
    qpos = qi * Q_BLOCK + lax.broadcasted_iota(I32, (1, Q_BLOCK), 1)

    halves = []
    for ref in (iqh_ref, iql_ref):
        iq_t = ref[...].astype(F32).T
        halves.append(jnp.concatenate(
            [iq_t[h * IDX_DIM:(h + 1) * IDX_DIM, :] for h in range(IDX_HEADS)], axis=1).astype(BF16))
    wi_ref[...] = jnp.concatenate([halves[0], halves[0], halves[1], jnp.zeros_like(halves[0])], axis=0)
    iw_t = ikw_ref[...].T
    q_t = qb_ref[...].astype(F32).T
    zrow = jnp.zeros((HEAD_DIM, 2 * Q_BLOCK), F32)
    for g in range(N_KV_HEADS):
        blk = jnp.concatenate([q_t[(2 * g + r) * HEAD_DIM:(2 * g + r + 1) * HEAD_DIM, :] for r in range(2)], axis=1)
        rows = [blk if gg == g else zrow for gg in range(N_KV_HEADS)]
        wq_ref[g] = jnp.concatenate(rows, axis=0).astype(BF16)

    def idx_body(c, _):
        r0 = pl.multiple_of(c * kc, kc)
        d = jnp.maximum(_dot(ikc_ref[pl.ds(r0, kc), :], wi_ref[...]), 0.0)
        sc = iw_t[IDX_DIM:IDX_DIM + 1, :] * d[:, 0:Q_BLOCK]
        for h in range(1, IDX_HEADS):
            sc = sc + iw_t[IDX_DIM + h:IDX_DIM + h + 1, :] * d[:, h * Q_BLOCK:(h + 1) * Q_BLOCK]
        kpos = c * kc + lax.broadcasted_iota(I32, (kc, Q_BLOCK), 0)
        sc = jnp.where(kpos <= qpos, sc * IDX_SCALE, -jnp.inf)
        key_ref[c] = _sort_key(sc)
        return 0

    lax.fori_loop(0, nkc, idx_body, 0)
    t = _topk_threshold_t(key_ref, nkc, kc, k_sel, idx_bits)

    m_ref[...] = jnp.full(m_ref.shape, NEG_BIG, F32)
    acc_ref[...] = jnp.zeros(acc_ref.shape, F32)

    def att_body(c, _):
        r0 = pl.multiple_of(c * kc, kc)
        kblk = kb_ref[pl.ds(r0, kc), :]
        bias_ref[...] = jnp.where(key_ref[c] >= t, 0.0, NEG_BIG)
        alphas = []
        for g in range(N_KV_HEADS):
            s = _dot(kblk, wq_ref[g])
            cmax = []
            for r in range(2):
                sr = s[:, r * Q_BLOCK:(r + 1) * Q_BLOCK] + bias_ref[...]
                s_ref[g, :, r * Q_BLOCK:(r + 1) * Q_BLOCK] = sr
                cmax.append(jnp.max(sr, axis=0, keepdims=True))
            m_old = m_ref[g]
            m_new = jnp.maximum(m_old, jnp.concatenate(cmax, axis=1))
            m_ref[g] = m_new
            alphas.append(jnp.exp2(m_old - m_new))
        for g in range(N_KV_HEADS):
            p = jnp.exp2(s_ref[g] - m_ref[g]).astype(BF16)
            vt = vt_ref[c, g * V_ROWS:(g + 1) * V_ROWS, :]
            acc_ref[g] = alphas[g] * acc_ref[g] + _dot(vt, p)
        return 0

    lax.fori_loop(0, nkc, att_body, 0)

    res_t = jnp.concatenate(
        [acc_ref[g, 0:HEAD_DIM, :] / acc_ref[g, HEAD_DIM:HEAD_DIM + 1, :] for g in range(N_KV_HEADS)],
        axis=0)
    res = res_t.T
    o_ref[...] = jnp.concatenate([res[0:Q_BLOCK], res[Q_BLOCK:]], axis=1)


def _dsa_prompt(qb, iqh, iql, h, kb, vt, ikc, nb, lt, kc):
    nq = lt // Q_BLOCK
    nck = lt // kc
    k_sel = min(TOPK_MAX, lt // 4)
    idx_bits = max(1, (lt - 1).bit_length())

    def qblk(width, off=0):
        return pl.BlockSpec((Q_BLOCK, width), lambda b, i: (b * nq + i, off // width))

    def seq(width):
        return pl.BlockSpec((lt, width), lambda b, i: (b, 0))

    return pl.pallas_call(
        functools.partial(_dsa_kernel, kc=kc, k_sel=k_sel, idx_bits=idx_bits),
        grid=(nb, nq),
        in_specs=[qblk(ATT_WIDTH), qblk(KV_WIDTH), qblk(KV_WIDTH), qblk(LANES, OFF_IKW),
                  seq(KV_WIDTH), pl.BlockSpec((nck, N_KV_HEADS * V_ROWS, kc), lambda b, i: (b, 0, 0)),
                  seq(KV_WIDTH)],
        out_specs=qblk(ATT_WIDTH),
        out_shape=jax.ShapeDtypeStruct((nb * lt, ATT_WIDTH), F32),
        scratch_shapes=[
            pltpu.VMEM((nck, kc, Q_BLOCK), I32),
            pltpu.VMEM((N_KV_HEADS, KV_WIDTH, 2 * Q_BLOCK), BF16),
            pltpu.VMEM((4 * IDX_DIM, IDX_HEADS * Q_BLOCK), BF16),
            pltpu.VMEM((N_KV_HEADS, kc, 2 * Q_BLOCK), F32),
            pltpu.VMEM((kc, Q_BLOCK), F32),
            pltpu.VMEM((N_KV_HEADS, 1, 2 * Q_BLOCK), F32),
            pltpu.VMEM((N_KV_HEADS, V_ROWS, 2 * Q_BLOCK), F32),
        ],
        compiler_params=_cparams("parallel", "arbitrary"),
        name="dsa_prompt",
    )(qb, iqh, iql, h, kb, vt, ikc)


S5_LANE_BLOCK = 512


S5_TILES = S5_WIDTH // LANES


def _s5_permute_in(u_ref, up_ref, tt):
    if tt == 1:
        return u_ref[:, 0, :]
    for j in range(SUBLANES):
        uj = u_ref[j]
        for k in range(S5_TILES):
            up_ref[k, pl.ds(j, tt, stride=SUBLANES), :] = uj[:, k * LANES:(k + 1) * LANES]
    return jnp.concatenate([up_ref[k] for k in range(S5_TILES)], axis=1)


def _s5_permute_out(res, y_ref, o_ref, tt):
    if tt == 1:
        o_ref[:, 0, :] = res
        return
    for k in range(S5_TILES):
        y_ref[k] = res[:, k * LANES:(k + 1) * LANES]
    for j in range(SUBLANES):
        o_ref[j] = jnp.concatenate(
            [y_ref[k, pl.ds(j, tt, stride=SUBLANES), :] for k in range(S5_TILES)], axis=1)


def _s5_scan(bu_ref, ar_ref, ai_ref, hr_ref, hi_ref, tt, store):
    for lb in range(S5_LANES // S5_LANE_BLOCK):
        re = slice(lb * S5_LANE_BLOCK, (lb + 1) * S5_LANE_BLOCK)
        im = slice(S5_LANES + lb * S5_LANE_BLOCK, S5_LANES + (lb + 1) * S5_LANE_BLOCK)
        ar = jnp.broadcast_to(ar_ref[:, re], (SUBLANES, S5_LANE_BLOCK))
        ai = jnp.broadcast_to(ai_ref[:, re], (SUBLANES, S5_LANE_BLOCK))

        def step(t, carry):
            hr, hi = carry
            r0 = pl.multiple_of(t * SUBLANES, SUBLANES)
            nhr = ar * hr - ai * hi + bu_ref[pl.ds(r0, SUBLANES), re]
            nhi = ar * hi + ai * hr + bu_ref[pl.ds(r0, SUBLANES), im]
            if store:
                bu_ref[pl.ds(r0, SUBLANES), re] = nhr
                bu_ref[pl.ds(r0, SUBLANES), im] = nhi
            return nhr, nhi

        hr, hi = lax.fori_loop(0, tt, step, (hr_ref[:, re], hi_ref[:, re]), unroll=min(tt, 8))
        hr_ref[:, re] = hr
        hi_ref[:, re] = hi


def _s5_ends_kernel(u_ref, bmat_ref, ar_ref, ai_ref, e_ref, up_ref, bu_ref, hr_ref, hi_ref, *, tt):
    i = pl.program_id(1)

    @pl.when(i == 0)
    def _():
        hr_ref[...] = jnp.zeros(hr_ref.shape, F32)
        hi_ref[...] = jnp.zeros(hi_ref.shape, F32)

    up = _s5_permute_in(u_ref, up_ref, tt)
    bu_ref[...] = _dot(up.astype(BF16), bmat_ref[...])
    _s5_scan(bu_ref, ar_ref, ai_ref, hr_ref, hi_ref, tt, store=False)

    @pl.when(i == pl.num_programs(1) - 1)
    def _():
        e_ref[:, 0:S5_LANES] = hr_ref[...]
        e_ref[:, S5_LANES:] = hi_ref[...]


def _s5_carry_kernel(e_ref, pr_ref, pi_ref, h0_ref):
    e = e_ref[...]
    pr = pr_ref[...]
    pi = pi_ref[...]
    cr = jnp.zeros((1, S5_LANES), F32)
    ci = jnp.zeros((1, S5_LANES), F32)
    for j in range(SUBLANES):
        h0_ref[j:j + 1, 0:S5_LANES] = cr
        h0_ref[j:j + 1, S5_LANES:] = ci
        er = e[j:j + 1, 0:S5_LANES]
        ei = e[j:j + 1, S5_LANES:]
        cr, ci = pr * cr - pi * ci + er, pr * ci + pi * cr + ei


def _s5_main_kernel(u_ref, h0_ref, bmat_ref, cmat_ref, ar_ref, ai_ref, d_ref, wglu_ref, o_ref, hl_ref,
                    up_ref, bu_ref, y_ref, hr_ref, hi_ref, *, tt):
    i = pl.program_id(1)

    @pl.when(i == 0)
    def _():
        hr_ref[...] = h0_ref[:, 0:S5_LANES]
        hi_ref[...] = h0_ref[:, S5_LANES:]

    up = _s5_permute_in(u_ref, up_ref, tt)
    bu_ref[...] = _dot(up.astype(BF16), bmat_ref[...])
    _s5_scan(bu_ref, ar_ref, ai_ref, hr_ref, hi_ref, tt, store=True)

    @pl.when(i == pl.num_programs(1) - 1)
    def _():
        hl_ref[:, 0:S5_LANES] = hr_ref[...]
        hl_ref[:, S5_LANES:] = hi_ref[...]

    y = _dot(bu_ref[...].astype(BF16), cmat_ref[...]) + d_ref[...] * up
    g = jax.nn.gelu(y)
    res = g * jax.nn.sigmoid(_dot(g.astype(BF16), wglu_ref[...]))
    _s5_permute_out(res, y_ref, o_ref, tt)


def _s5_specs(nb, lseg, tt, col):
    u_spec = pl.BlockSpec((None, SUBLANES, tt, S5_WIDTH), lambda b, i: (b, 0, i, col))
    st_spec = pl.BlockSpec((None, SUBLANES, 2 * S5_LANES), lambda b, i: (b, 0, 0))

    def full(shape):
        return pl.BlockSpec(shape, lambda b, i: (0,) * len(shape))

    return u_spec, st_spec, full


def _s5_ends(u4, col, bmat, ar, ai, tt):
    nb, _, lseg, _ = u4.shape
    u_spec, st_spec, full = _s5_specs(nb, lseg, tt, col)
    return pl.pallas_call(
        functools.partial(_s5_ends_kernel, tt=tt),
        grid=(nb, lseg // tt),
        in_specs=[u_spec, full((S5_WIDTH, 2 * S5_LANES)), full((1, S5_LANES)), full((1, S5_LANES))],
        out_specs=st_spec,
        out_shape=jax.ShapeDtypeStruct((nb, SUBLANES, 2 * S5_LANES), F32),
        scratch_shapes=[
            pltpu.VMEM((S5_TILES, tt * SUBLANES, LANES), F32),
            pltpu.VMEM((tt * SUBLANES, 2 * S5_LANES), F32),
            pltpu.VMEM((SUBLANES, S5_LANES), F32),
            pltpu.VMEM((SUBLANES, S5_LANES), F32),
        ],
        compiler_params=_cparams("parallel", "arbitrary"),
        name="s5_ends",
    )(u4, bmat, ar, ai)


def _s5_carry(e, pr, pi):
    nb = e.shape[0]
    st_spec = pl.BlockSpec((None, SUBLANES, 2 * S5_LANES), lambda b: (b, 0, 0))
    p_spec = pl.BlockSpec((1, S5_LANES), lambda b: (0, 0))
    return pl.pallas_call(
        _s5_carry_kernel,
        grid=(nb,),
        in_specs=[st_spec, p_spec, p_spec],
        out_specs=st_spec,
        out_shape=jax.ShapeDtypeStruct(e.shape, F32),
        compiler_params=_cparams("parallel"),
        name="s5_carry",
    )(e, pr, pi)


def _s5_main(u4, col, h0, bmat, cmat, ar, ai, dskip, wglu, tt):
    nb, _, lseg, _ = u4.shape
    u_spec, st_spec, full = _s5_specs(nb, lseg, tt, col)
    o_spec = pl.BlockSpec((None, SUBLANES, tt, S5_WIDTH), lambda b, i: (b, 0, i, 0))
    return pl.pallas_call(
        functools.partial(_s5_main_kernel, tt=tt),
        grid=(nb, lseg // tt),
        in_specs=[u_spec, st_spec, full((S5_WIDTH, 2 * S5_LANES)), full((2 * S5_LANES, S5_WIDTH)),
                  full((1, S5_LANES)), full((1, S5_LANES)), full((1, S5_WIDTH)), full((S5_WIDTH, S5_WIDTH))],
        out_specs=[o_spec, st_spec],
        out_shape=[jax.ShapeDtypeStruct((nb, SUBLANES, lseg, S5_WIDTH), F32),
                   jax.ShapeDtypeStruct((nb, SUBLANES, 2 * S5_LANES), F32)],
        scratch_shapes=[
            pltpu.VMEM((S5_TILES, tt * SUBLANES, LANES), F32),
            pltpu.VMEM((tt * SUBLANES, 2 * S5_LANES), F32),
            pltpu.VMEM((S5_TILES, tt * SUBLANES, LANES), F32),
            pltpu.VMEM((SUBLANES, S5_LANES), F32),
            pltpu.VMEM((SUBLANES, S5_LANES), F32),
        ],
        compiler_params=_cparams("parallel", "arbitrary"),
        name="s5_main",
    )(u4, h0, bmat, cmat, ar, ai, dskip, wglu)


def _softplus(x):
    return jnp.maximum(x, 0.0) + jnp.log1p(jnp.exp(-jnp.abs(x)))


def _silu(x):
    return x * jax.nn.sigmoid(x)


def _ssd_gate_norm(y, z, nrm):
    y = y * _silu(z)
    outs = []
    gw = SSM_INNER // SSM_GROUPS
    for g in range(SSM_GROUPS):
        yg = y[:, g * gw:(g + 1) * gw]
        ms = jnp.mean(yg * yg, axis=-1, keepdims=True)
        outs.append(yg * lax.rsqrt(ms + EPS))
    return jnp.concatenate(outs, axis=1) * nrm


def _ssd_kernel(xbc_ref, z_ref, dt_ref, cw_ref, cb_ref, dtb_ref, alog_ref, dexp_ref, nrm_ref, exp_ref,
                o_ref, st_ref, xpad_ref, stt_ref):
    c = pl.program_id(1)
    t = SSD_CHUNK
    gw = SSM_INNER // SSM_GROUPS

    @pl.when(c == 0)
    def _():
        xpad_ref[0:SUBLANES, :] = jnp.zeros((SUBLANES, CONV_DIM), F32)
        stt_ref[...] = jnp.zeros(stt_ref.shape, F32)

    x = xbc_ref[...]
    xpad_ref[SUBLANES:SUBLANES + t, :] = x
    conv = cb_ref[...]
    for k in range(CONV_WIDTH):
        conv = conv + xpad_ref[pl.ds(SUBLANES - (CONV_WIDTH - 1) + k, t), :] * cw_ref[k:k + 1, :]
    xpad_ref[0:SUBLANES, :] = x[t - SUBLANES:t, :]
    act = _silu(conv)
    xs = act[:, 0:SSM_INNER]
    bm = act[:, SSM_INNER:SSM_INNER + SSM_GROUPS * SSM_STATE]
    cm = act[:, SSM_INNER + SSM_GROUPS * SSM_STATE:]

    dt = _softplus(dt_ref[...] + dtb_ref[...])
    a = -jnp.exp(alog_ref[...])
    row = lax.broadcasted_iota(I32, (t, t), 0)
    colm = lax.broadcasted_iota(I32, (t, t), 1)
    causal = row >= colm
    ltri = jnp.where(causal, 1.0, 0.0).astype(BF16)
    cs = _dot3_exact_lhs(ltri, dt * a)
    cs_t = cs.T
    dt_t = dt.T
    ecs = jnp.exp(cs)
    wgt = jnp.exp(cs[t - 1:t, :] - cs) * dt
    expand = exp_ref[...]
    ecs_x = _dot3_exact_rhs(ecs, expand)
    wgt_x = _dot3_exact_rhs(wgt, expand)

    lane = lax.broadcasted_iota(I32, (t, LANES), 1)
    lo_half = lane < SSM_HEAD_DIM
    xs_b = xs.astype(BF16)
    zero_b = jnp.zeros((t, LANES), BF16)
    y_tiles = []
    for hp in range(SSM_HEADS // 2):
        g = hp // (SSM_HEADS // 2 // SSM_GROUPS)
        cb = _dot_nt(cm[:, g * SSM_STATE:(g + 1) * SSM_STATE].astype(BF16),
                     bm[:, g * SSM_STATE:(g + 1) * SSM_STATE].astype(BF16))
        xt = xs_b[:, hp * LANES:(hp + 1) * LANES]
        acc = None
        for hh in range(2):
            h = 2 * hp + hh
            seg = cs[:, h:h + 1] - cs_t[h:h + 1, :]
            dec = jnp.exp(jnp.where(causal, seg, -jnp.inf))
            w = (cb * dec * dt_t[h:h + 1, :]).astype(BF16)
            xm = jnp.where(lo_half if hh == 0 else ~lo_half, xt, zero_b)
            part = _dot(w, xm)
            acc = part if acc is None else acc + part
        y_tiles.append(acc)
    y = jnp.concatenate(y_tiles, axis=1)

    xw = (xs * wgt_x).astype(BF16)
    cdec = ecs_x[t - 1:t, :]
    y_off = []
    for g in range(SSM_GROUPS):
        st = stt_ref[g]
        cg = cm[:, g * SSM_STATE:(g + 1) * SSM_STATE].astype(BF16)
        y_off.append(_dot(cg, st.astype(BF16)) * ecs_x[:, g * gw:(g + 1) * gw])
        bg = bm[:, g * SSM_STATE:(g + 1) * SSM_STATE].astype(BF16)
        stt_ref[g] = st * cdec[:, g * gw:(g + 1) * gw] + _dot_tn(bg, xw[:, g * gw:(g + 1) * gw])
    y = y + jnp.concatenate(y_off, axis=1) + dexp_ref[...] * xs
    o_ref[...] = _ssd_gate_norm(y, z_ref[...], nrm_ref[...])

    @pl.when(c == pl.num_programs(1) - 1)
    def _():
        for g in range(SSM_GROUPS):
            st_ref[g * gw:(g + 1) * gw, :] = stt_ref[g].T


def _ssd_prompt(h, cw, cb, dtb, alog, dexp, nrm, expand, nb, lt):
    nc = lt // SSD_CHUNK

    def col(width, off):
        return pl.BlockSpec((SSD_CHUNK, width), lambda b, c: (b * nc + c, off // width))

    def full(shape):
        return pl.BlockSpec(shape, lambda b, c: (0, 0))

    return pl.pallas_call(
        _ssd_kernel,
        grid=(nb, nc),
        in_specs=[col(CONV_DIM, OFF_XBC), col(SSM_INNER, OFF_Z), col(LANES, OFF_DT),
                  full((CONV_WIDTH, CONV_DIM)), full((1, CONV_DIM)), full((1, LANES)), full((1, LANES)),
                  full((1, SSM_INNER)), full((1, SSM_INNER)), full((LANES, SSM_INNER))],
        out_specs=[pl.BlockSpec((SSD_CHUNK, SSM_INNER), lambda b, c: (b * nc + c, 0)),
                   pl.BlockSpec((None, SSM_INNER, SSM_STATE), lambda b, c: (b, 0, 0))],
        out_shape=[jax.ShapeDtypeStruct((nb * lt, SSM_INNER), F32),
                   jax.ShapeDtypeStruct((nb, SSM_INNER, SSM_STATE), F32)],
        scratch_shapes=[pltpu.VMEM((SUBLANES + SSD_CHUNK, CONV_DIM), F32),
                        pltpu.VMEM((SSM_GROUPS, SSM_STATE, SSM_INNER // SSM_GROUPS), F32)],
        compiler_params=_cparams("parallel", "arbitrary"),
        name="ssd_prompt",
    )(h, h, h, cw, cb, dtb, alog, dexp, nrm, expand)


def _merge_kernel(x_ref, g0_ref, g1_ref, g2_ref, os5_ref, oatt_ref, ossm_ref, w1_ref, w2_ref, w3_ref, wo_ref,
                  o_ref):
    merged = (jax.nn.sigmoid(g0_ref[...]) * _dot(os5_ref[...].astype(BF16), w1_ref[...])
              + jax.nn.sigmoid(g1_ref[...]) * _dot(oatt_ref[...].astype(BF16), w2_ref[...])
              + jax.nn.sigmoid(g2_ref[...]) * _dot(ossm_ref[...].astype(BF16), w3_ref[...]))
    o_ref[...] = x_ref[...] + _dot(merged.astype(BF16), wo_ref[...])


def _merge(x, h, os5, oatt, ossm, w1, w2, w3, wo, tm):
    m = x.shape[0]

    def rows(width, blk=0):
        return pl.BlockSpec((tm, width), lambda i: (i, blk))

    def full(shape):
        return pl.BlockSpec(shape, lambda i: (0, 0))

    return pl.pallas_call(
        _merge_kernel,
        grid=(m // tm,),
        in_specs=[rows(D_MODEL), rows(D_MODEL, 0), rows(D_MODEL, 1), rows(D_MODEL, 2),
                  rows(S5_WIDTH), rows(ATT_WIDTH), rows(SSM_INNER),
                  full((S5_WIDTH, D_MODEL)), full((ATT_WIDTH, D_MODEL)), full((SSM_INNER, D_MODEL)),
                  full((D_MODEL, D_MODEL))],
        out_specs=rows(D_MODEL),
        out_shape=jax.ShapeDtypeStruct((m, D_MODEL), F32),
        compiler_params=_cparams("parallel"),
        name="merge",
    )(x, h, h, h, os5, oatt, ossm, w1, w2, w3, wo)


def _mlp_kernel(x_ref, g_ref, wu_ref, wd_ref, o_ref):
    x = x_ref[...]
    ms = jnp.mean(x * x, axis=-1, keepdims=True)
    xn = (x * lax.rsqrt(ms + EPS) * g_ref[...]).astype(BF16)
    hid = jnp.maximum(_dot(xn, wu_ref[...]), 0.0)
    o_ref[...] = x + _dot((hid * hid).astype(BF16), wd_ref[...])


def _mlp(x, g, wu, wd, tm):
    m = x.shape[0]
    return pl.pallas_call(
        _mlp_kernel,
        grid=(m // tm,),
        in_specs=[pl.BlockSpec((tm, D_MODEL), lambda i: (i, 0)),
                  pl.BlockSpec((1, D_MODEL), lambda i: (0, 0)),
                  pl.BlockSpec((D_MODEL, D_FF), lambda i: (0, 0)),
                  pl.BlockSpec((D_FF, D_MODEL), lambda i: (0, 0))],
        out_specs=pl.BlockSpec((tm, D_MODEL), lambda i: (i, 0)),
        out_shape=jax.ShapeDtypeStruct((m, D_MODEL), F32),
        compiler_params=_cparams("parallel"),
        name="mlp",
    )(x, g, wu, wd)


def _ssd_step_kernel(xbc_ref, z_ref, dt_ref, cst_ref, st_ref, cw_ref, cb_ref, dtb_ref, alog_ref, dexp_ref,
                     nrm_ref, exp_ref, o_ref, cst_out_ref, st_out_ref):
    ns = SUBLANES
    gw = SSM_INNER // SSM_GROUPS
    x = xbc_ref[...]
    conv = cb_ref[...]
    for k in range(CONV_WIDTH - 1):
        conv = conv + cst_ref[k] * cw_ref[k:k + 1, :]
    conv = conv + x * cw_ref[CONV_WIDTH - 1:CONV_WIDTH, :]
    for k in range(CONV_WIDTH - 2):
        cst_out_ref[k] = cst_ref[k + 1]
    cst_out_ref[CONV_WIDTH - 2] = x
    act = _silu(conv)
    xs = act[:, 0:SSM_INNER]
    bm = act[:, SSM_INNER:SSM_INNER + SSM_GROUPS * SSM_STATE]
    cm = act[:, SSM_INNER + SSM_GROUPS * SSM_STATE:]
    dt = _softplus(dt_ref[...] + dtb_ref[...])
    decay = jnp.exp(dt * (-jnp.exp(alog_ref[...])))
    expand = exp_ref[...]
    decay_x = _dot3_exact_rhs(decay, expand)
    xdt = xs * _dot3_exact_rhs(dt, expand)
    rowid = lax.broadcasted_iota(I32, (ns, 1), 0)
    ones = jnp.ones((ns, SSM_STATE), BF16)
    y = jnp.zeros((ns, SSM_INNER), F32)
    y_parts = [[], []]
    for s in range(ns):
        pick = rowid == s
        for g in range(SSM_GROUPS):
            cols = slice(g * gw, (g + 1) * gw)
            nsl = slice(g * SSM_STATE, (g + 1) * SSM_STATE)
            dsel = jnp.where(pick, decay_x[:, cols], 0.0)
            dh, dm, dl = _split3(dsel)
            dmat = _dot_tn(dh, ones) + _dot_tn(dm, ones) + _dot_tn(dl, ones)
            xsel = jnp.where(pick, xdt[:, cols], 0.0)
            xh = xsel.astype(BF16)
            xl = (xsel - xh.astype(F32)).astype(BF16)
            bg = bm[:, nsl].astype(BF16)
            new = dmat * st_ref[s, cols, :] + _dot_tn(xh, bg) + _dot_tn(xl, bg)
            st_out_ref[s, cols, :] = new
            csel = jnp.where(pick, cm[:, nsl], 0.0).astype(BF16)
            y_parts[g].append(_dot_nt(csel, new.astype(BF16)))
    halves = []
    for g in range(SSM_GROUPS):
        acc = y_parts[g][0]
        for part in y_parts[g][1:]:
            acc = acc + part
        halves.append(acc)
    y = jnp.concatenate(halves, axis=1) + dexp_ref[...] * xs
    o_ref[...] = _ssd_gate_norm(y, z_ref[...], nrm_ref[...])


def _ssd_step(h, cst, st, cw, cb, dtb, alog, dexp, nrm, expand):
    ns = h.shape[0]
    g8 = SUBLANES

    def col(width, off):
        return pl.BlockSpec((g8, width), lambda i: (i, off // width))

    def full(shape):
        return pl.BlockSpec(shape, lambda i: (0, 0))

    cst_spec = pl.BlockSpec((CONV_WIDTH - 1, g8, CONV_DIM), lambda i: (0, i, 0))
    st_spec = pl.BlockSpec((g8, SSM_INNER, SSM_STATE), lambda i: (i, 0, 0))
    return pl.pallas_call(
        _ssd_step_kernel,
        grid=(ns // g8,),
        in_specs=[col(CONV_DIM, OFF_XBC), col(SSM_INNER, OFF_Z), col(LANES, OFF_DT), cst_spec, st_spec,
                  full((CONV_WIDTH, CONV_DIM)), full((1, CONV_DIM)), full((1, LANES)), full((1, LANES)),
                  full((1, SSM_INNER)), full((1, SSM_INNER)), full((LANES, SSM_INNER))],
        out_specs=[pl.BlockSpec((g8, SSM_INNER), lambda i: (i, 0)), cst_spec, st_spec],
        out_shape=[jax.ShapeDtypeStruct((ns, SSM_INNER), F32),
                   jax.ShapeDtypeStruct(cst.shape, F32),
                   jax.ShapeDtypeStruct(st.shape, F32)],
        compiler_params=_cparams("parallel"),
        name="ssd_step",
    )(h, h, h, cst, st, cw, cb, dtb, alog, dexp, nrm, expand)


PAGES_PER_CHUNK = 16
SAMPLE_KC = PAGES_PER_CHUNK * PAGE_SIZE
N_SLOTS = 256


def _selected_positions(sel2, n_slots):
    n_p = sel2.shape[0]
    sel_b = sel2.astype(BF16)
    oi = lax.broadcasted_iota(I32, (PAGE_SIZE, PAGE_SIZE), 0)
    oj = lax.broadcasted_iota(I32, (PAGE_SIZE, PAGE_SIZE), 1)
    inrow = _dot(sel_b, jnp.where(oi <= oj, 1.0, 0.0).astype(BF16))
    rowtot = _dot(sel_b, jnp.ones((PAGE_SIZE, PAGE_SIZE), BF16))
    pi = lax.broadcasted_iota(I32, (n_p, n_p), 0)
    pj = lax.broadcasted_iota(I32, (n_p, n_p), 1)
    rowpre = _dot(jnp.where(pj < pi, 1.0, 0.0).astype(BF16), rowtot.astype(BF16))
    cum = rowpre + inrow
    rowcum = (rowpre + rowtot)[:, 0:1]
    slot = lax.broadcasted_iota(I32, (1, n_slots), 1).astype(F32)
    page_s = jnp.sum(jnp.where(rowcum <= slot, 1.0, 0.0), axis=0, keepdims=True)
    prow = lax.broadcasted_iota(I32, (n_p, n_slots), 0).astype(F32)
    onehot_t = jnp.where(prow == page_s, 1.0, 0.0).astype(BF16)
    cum_of_page = _dot(cum.T.astype(BF16), onehot_t)
    off_s = jnp.sum(jnp.where(cum_of_page <= slot, 1.0, 0.0), axis=0, keepdims=True)
    return page_s * PAGE_SIZE + off_s


def _dsa_sample_kernel(pt_ref, q_ref, iqh_ref, iql_ref, ikw_ref, kn_ref, vn_ref, ikn_ref,
                       ck_hbm, cv_hbm, cik_hbm, o_ref,
                       ikbuf, kg, vg, key_ref, pos_vmem, pos_smem, sem_ik, sem_kv, sem_pos,
                       *, n_pages, n_pool, layer, k_sel, idx_bits):
    b = pl.program_id(0)
    kc = SAMPLE_KC
    n_chunks = n_pages // PAGES_PER_CHUNK
    past = n_pages * PAGE_SIZE

    def ik_copy(p):
        return pltpu.make_async_copy(cik_hbm.at[layer * n_pool + pt_ref[b, p]], ikbuf.at[p], sem_ik)

    def start_ik(p, _):
        ik_copy(p).start()
        return 0

    def wait_ik(p, _):
        ik_copy(p).wait()
        return 0

    lax.fori_loop(0, n_pages, start_ik, 0)
    lax.fori_loop(0, n_pages, wait_ik, 0)

    def stack(parts):
        zero = jnp.zeros((SUBLANES - IDX_HEADS, IDX_DIM), F32)
        cols = [jnp.concatenate([p[:, h * IDX_DIM:(h + 1) * IDX_DIM].astype(F32) for h in range(IDX_HEADS)]
                                + [zero], axis=0) for p in parts]
        return jnp.concatenate(cols + [jnp.zeros((SUBLANES, IDX_DIM), F32)], axis=1).astype(BF16)

    iqh = iqh_ref[...]
    iql = iql_ref[...]
    lhs = stack([iqh, iqh, iql])
    iw = ikw_ref[...]
    iw_col = jnp.concatenate([iw[:, IDX_DIM + h:IDX_DIM + h + 1] for h in range(IDX_HEADS)]
                             + [jnp.zeros((SUBLANES - IDX_HEADS, 1), F32)], axis=0)

    def score_of(ik):
        x = jnp.concatenate([ik, jnp.zeros_like(ik)], axis=1)
        kh = x.astype(BF16)
        kl = pltpu.roll(x - kh.astype(F32), IDX_DIM, 1).astype(BF16)
        d = jnp.maximum(_dot_nt(lhs, jnp.concatenate([kh + kl, kh], axis=1)), 0.0)
        return jnp.sum(iw_col * d, axis=0, keepdims=True) * IDX_SCALE

    for c in range(n_chunks):
        sc = score_of(ikbuf[c * PAGES_PER_CHUNK:(c + 1) * PAGES_PER_CHUNK].reshape(kc, IDX_DIM))
        key_c = _sort_key(sc)
        for i in range(PAGES_PER_CHUNK):
            p = c * PAGES_PER_CHUNK + i
            key_ref[p:p + 1, :] = key_c[:, i * PAGE_SIZE:(i + 1) * PAGE_SIZE]
    ikn = ikn_ref[...]
    s_new = score_of(jnp.concatenate([ikn, jnp.zeros((SUBLANES - 1, IDX_DIM), F32)], axis=0))[:, 0:1]
    tail = (lax.broadcasted_iota(I32, (SUBLANES, LANES), 0) == 0) & (
        lax.broadcasted_iota(I32, (SUBLANES, LANES), 1) == 0)
    key_ref[n_pages:n_pages + SUBLANES, :] = _sort_key(jnp.where(tail, s_new, -jnp.inf))

    member = _topk_select_single(key_ref[...], k_sel, idx_bits)
    sel_all = jnp.where(member, 1.0, 0.0)
    sel2 = sel_all[0:n_pages, :]
    sel_new = sel_all[n_pages:n_pages + 1, 0:1] > 0.5
    n_valid = jnp.sum(sel2).astype(I32)
    pos_vmem[...] = jnp.broadcast_to(_selected_positions(sel2, N_SLOTS).astype(I32), pos_vmem.shape)
    to_smem = pltpu.make_async_copy(pos_vmem, pos_smem, sem_pos)
    to_smem.start()
    to_smem.wait()

    def row_copies(s):
        pos = pos_smem[0, s]
        pos = jnp.where(pos < past, pos, 0)
        page = layer * n_pool + pt_ref[b, lax.shift_right_logical(pos, PAGE_SIZE.bit_length() - 1)]
        off = pos & (PAGE_SIZE - 1)
        return (pltpu.make_async_copy(ck_hbm.at[page, off], kg.at[s], sem_kv.at[0]),
                pltpu.make_async_copy(cv_hbm.at[page, off], vg.at[s], sem_kv.at[1]))

    def start_rows(s, _):
        ck, cv = row_copies(s)
        ck.start()
        cv.start()
        return 0

    def wait_rows(s, _):
        ck, cv = row_copies(s)
        ck.wait()
        cv.wait()
        return 0

    lax.fori_loop(0, N_SLOTS, start_rows, 0)
    lax.fori_loop(0, N_SLOTS, wait_rows, 0)

    q = q_ref[...].astype(F32)
    rows = []
    for h in range(N_HEADS):
        g = h // 2
        tile = q[:, g * LANES:(g + 1) * LANES]
        src = tile if (g % 2) == (h % 2) else pltpu.roll(tile, HEAD_DIM, 1)
        lo = (g % 2) * HEAD_DIM
        lane_t = lax.broadcasted_iota(I32, (1, LANES), 1)
        placed = jnp.where((lane_t >= lo) & (lane_t < lo + HEAD_DIM), src, 0.0)
        zero = jnp.zeros((1, LANES), F32)
        rows.append(jnp.concatenate([placed, zero] if g < 2 else [zero, placed], axis=1))
    qrows_b = jnp.concatenate(rows, axis=0).astype(BF16)
    lo_lane = (lax.broadcasted_iota(I32, (N_HEADS, KV_WIDTH), 0) // 2) * HEAD_DIM
    lane8 = lax.broadcasted_iota(I32, (N_HEADS, KV_WIDTH), 1)
    keep = (lane8 >= lo_lane) & (lane8 < lo_lane + HEAD_DIM)

    slot = lax.broadcasted_iota(I32, (1, N_SLOTS), 1)
    s = jnp.where(slot < n_valid, _dot_nt(qrows_b, kg[...].astype(BF16)), NEG_BIG)
    kn = kn_ref[...].astype(F32)
    vn = vn_ref[...].astype(F32)
    s_n = jnp.where(sel_new, jnp.sum(qrows_b.astype(F32) * kn, axis=1, keepdims=True), NEG_BIG)
    m = jnp.maximum(jnp.max(s, axis=1, keepdims=True), s_n)
    p = jnp.exp2(s - m)
    p_n = jnp.exp2(s_n - m)
    denom = jnp.sum(p, axis=1, keepdims=True) + p_n
    acc = _dot(p.astype(BF16), vg[...].astype(BF16)) + p_n * vn
    res = jnp.where(keep, acc / denom, 0.0)
    out_r = []
    for r in range(2):
        row = res[r:r + 1, :]
        for g in range(1, N_KV_HEADS):
            row = row + res[2 * g + r:2 * g + r + 1, :]
        out_r.append(row)
    o_ref[...] = jnp.concatenate(out_r, axis=1)


def _dsa_sample(page_table, qb, iqh, iql, h, kb, vb, iko, cache_k, cache_v, cache_ik, layer):
    ns, n_pages = page_table.shape
    depth, n_pool = cache_k.shape[0], cache_k.shape[1]
    total = n_pages * PAGE_SIZE + 1
    k_sel = min(TOPK_MAX, total // 4)
    assert k_sel <= N_SLOTS and n_pages % PAGES_PER_CHUNK == 0
    idx_bits = max(1, (total - 1).bit_length())

    def row(width):
        return pl.BlockSpec((None, 1, width), lambda b, pt: (b, 0, 0))

    any_spec = pl.BlockSpec(memory_space=pl.ANY)
    ikw = h[:, OFF_IKW:OFF_IKW + LANES]
    grid_spec = pltpu.PrefetchScalarGridSpec(
        num_scalar_prefetch=1,
        grid=(ns,),
        in_specs=[row(ATT_WIDTH), row(KV_WIDTH), row(KV_WIDTH), row(LANES), row(KV_WIDTH), row(KV_WIDTH),
                  row(IDX_DIM), any_spec, any_spec, any_spec],
        out_specs=row(ATT_WIDTH),
        scratch_shapes=[
            pltpu.VMEM((n_pages, PAGE_SIZE, IDX_DIM), F32),
            pltpu.VMEM((N_SLOTS, KV_WIDTH), F32),
            pltpu.VMEM((N_SLOTS, KV_WIDTH), F32),
            pltpu.VMEM((n_pages + SUBLANES, PAGE_SIZE), I32),
            pltpu.VMEM((SUBLANES, N_SLOTS), I32),
            pltpu.SMEM((SUBLANES, N_SLOTS), I32),
            pltpu.SemaphoreType.DMA(()),
            pltpu.SemaphoreType.DMA((2,)),
            pltpu.SemaphoreType.DMA(()),
        ],
    )
    r3 = lambda a: a.reshape(ns, 1, a.shape[-1])
    out = pl.pallas_call(
        functools.partial(_dsa_sample_kernel, n_pages=n_pages, n_pool=n_pool, layer=layer, k_sel=k_sel,
                          idx_bits=idx_bits),
        grid_spec=grid_spec,
        out_shape=jax.ShapeDtypeStruct((ns, 1, ATT_WIDTH), F32),
        compiler_params=_cparams("arbitrary"),
        name="dsa_sample",
    )(page_table, r3(qb), r3(iqh), r3(iql), r3(ikw), r3(kb), r3(vb), r3(iko),
      cache_k.reshape(depth * n_pool, PAGE_SIZE, KV_WIDTH), cache_v.reshape(depth * n_pool, PAGE_SIZE, KV_WIDTH),
      cache_ik.reshape(depth * n_pool, PAGE_SIZE, IDX_DIM))
    return out.reshape(ns, ATT_WIDTH)


def _complex_pow(re, im, n):
    rr, ri = jnp.ones_like(re), jnp.zeros_like(im)
    while n:
        if n & 1:
            rr, ri = rr * re - ri * im, rr * im + ri * re
        re, im = re * re - im * im, 2.0 * re * im
        n >>= 1
    return rr, ri


def _layer_params(lw, lseg):
    w = lw["w_in"]
    offs = [0]
    for s in IN_SIZES:
        offs.append(offs[-1] + s)
    u_, q_, k_, v_, iq_, ik_, iw_, z_, xbc_, dt_, gt_ = [w[:, offs[i]:offs[i + 1]] for i in range(len(IN_SIZES))]
    zpad = lambda n: jnp.zeros((D_MODEL, n), w.dtype)
    wp = jnp.concatenate([gt_, z_, u_, xbc_, q_, k_, v_, iq_, ik_, iw_, zpad(LANES - IDX_DIM - IDX_HEADS),
                          dt_, zpad(LANES - SSM_HEADS)], axis=1).astype(BF16)
    p = {"wp": wp, "norm_mix": lw["norm_mix"].reshape(1, D_MODEL)}
    p["qg"] = jnp.tile(lw["q_norm"], N_HEADS).reshape(1, ATT_WIDTH)
    p["kg"] = jnp.tile(lw["k_norm"], N_KV_HEADS).reshape(1, KV_WIDTH)
    ones = jnp.ones((HEAD_DIM, HEAD_DIM), BF16)
    p["bdq"] = jnp.kron(jnp.eye(N_HEADS, dtype=BF16), ones)
    p["bdk"] = jnp.kron(jnp.eye(N_KV_HEADS, dtype=BF16), ones)

    ar, ai = lw["s5_a_re"], lw["s5_a_im"]
    dt = jnp.exp(lw["s5_log_dt"])[:, None]
    mag = jnp.exp(dt * ar)
    abar_re, abar_im = mag * jnp.cos(dt * ai), mag * jnp.sin(dt * ai)
    den = ar * ar + ai * ai
    nr, ni = abar_re - 1.0, abar_im
    coef_re = (nr * ar + ni * ai) / den
    coef_im = (ni * ar - nr * ai) / den
    bc_re = coef_re[:, :, None] * lw["s5_b_re"] - coef_im[:, :, None] * lw["s5_b_im"]
    bc_im = coef_re[:, :, None] * lw["s5_b_im"] + coef_im[:, :, None] * lw["s5_b_re"]
    eye = jnp.eye(S5_GROUPS, dtype=F32)
    block_diag = lambda a: (jnp.swapaxes(a, 1, 2)[:, :, None, :] * eye[:, None, :, None]).reshape(
        a.shape[0] * a.shape[2], a.shape[0] * a.shape[1])
    embed_b = block_diag
    embed_c = block_diag
    p["bmat"] = jnp.concatenate([embed_b(bc_re), embed_b(bc_im)], axis=1).astype(BF16)
    p["cmat"] = jnp.concatenate([embed_c(lw["s5_c_re"]), -embed_c(lw["s5_c_im"])], axis=0).astype(BF16)
    p["ar"] = abar_re.reshape(1, S5_LANES)
    p["ai"] = abar_im.reshape(1, S5_LANES)
    pr, pi = _complex_pow(abar_re, abar_im, lseg)
    p["pr"] = pr.reshape(1, S5_LANES)
    p["pi"] = pi.reshape(1, S5_LANES)
    p["dskip"] = lw["s5_d"].reshape(1, S5_WIDTH)
    p["wglu"] = lw["s5_w_glu"].astype(BF16)

    p["cw"] = lw["conv_w"]
    p["cb"] = lw["conv_b"].reshape(1, CONV_DIM)
    padh = lambda v: jnp.pad(v, (0, LANES - SSM_HEADS)).reshape(1, LANES)
    p["dtb"] = padh(lw["dt_bias"])
    p["alog"] = padh(lw["a_log"])
    p["dexp"] = jnp.repeat(lw["ssm_d"], SSM_HEAD_DIM).reshape(1, SSM_INNER)
    p["nrm"] = lw["ssm_norm"].reshape(1, SSM_INNER)
    p["expand"] = jnp.pad(jnp.kron(jnp.eye(SSM_HEADS, dtype=F32), jnp.ones((1, SSM_HEAD_DIM), F32)),
                          ((0, LANES - SSM_HEADS), (0, 0))).astype(BF16)

    p["w1"] = lw["w_br_s5"].astype(BF16)
    p["w2"] = (lw["w_br_att"].reshape(N_KV_HEADS, 2, HEAD_DIM, D_MODEL).transpose(1, 0, 2, 3)
               .reshape(ATT_WIDTH, D_MODEL).astype(BF16))
    p["w3"] = lw["w_br_ssm"].astype(BF16)
    p["wo"] = lw["w_o"].astype(BF16)
    p["norm_mlp"] = lw["norm_mlp"].reshape(1, D_MODEL)
    p["wu"] = lw["w_up"].astype(BF16)
    p["wd"] = lw["w_down"].astype(BF16)
    return p


def _rope_table(pos):
    half = ROT_DIM // 2
    inv_freq = ROPE_THETA ** (-jnp.arange(half, dtype=F32) / half)
    ang = pos.astype(F32)[:, None] * inv_freq[None, :]
    cos, sin = jnp.cos(ang), jnp.sin(ang)
    n = pos.shape[0]
    rest = HEAD_DIM - ROT_DIM
    c = jnp.concatenate([cos, cos, jnp.ones((n, rest), F32)], axis=1)
    sa = jnp.concatenate([-sin, jnp.zeros((n, half + rest), F32)], axis=1)
    sb = jnp.concatenate([jnp.zeros((n, half), F32), sin, jnp.zeros((n, rest), F32)], axis=1)
    rep = LANES // HEAD_DIM
    return jnp.concatenate([jnp.tile(c, (1, rep)), jnp.tile(sa, (1, rep)), jnp.tile(sb, (1, rep))], axis=1)


def _pick(n, pref):
    for t in pref:
        if n % t == 0:
            return t
    return n


def _layer_prompt(x2, p, tab, nb, lt):
    m = nb * lt
    lseg = lt // SUBLANES
    h = _inproj(x2, p["norm_mix"], p["wp"], _pick(m, (1024, 512, 256, 128)))
    kc = _pick(lt, (512, 256, 128))
    qb, ko, kb, vo, vt, iqh, iql, iko, ikc = _post(
        h, tab, p["qg"], p["kg"], p["bdq"], p["bdk"], nb, lt, kc, True)
    oatt = _dsa_prompt(qb, iqh, iql, h, kb, vt, ikc, nb, lt, kc)

    u4 = h.reshape(nb, SUBLANES, lseg, NP)
    tt = _pick(lseg, (64, 32, 16, 8))
    ends = _s5_ends(u4, OFF_U // S5_WIDTH, p["bmat"], p["ar"], p["ai"], tt)
    h0 = _s5_carry(ends, p["pr"], p["pi"])
    os5, hl = _s5_main(u4, OFF_U // S5_WIDTH, h0, p["bmat"], p["cmat"], p["ar"], p["ai"], p["dskip"],
                       p["wglu"], tt)

    ossm, st = _ssd_prompt(h, p["cw"], p["cb"], p["dtb"], p["alog"], p["dexp"], p["nrm"], p["expand"], nb, lt)

    tm = _pick(m, (512, 256, 128))
    x1 = _merge(x2, h, os5.reshape(m, S5_WIDTH), oatt, ossm, p["w1"], p["w2"], p["w3"], p["wo"], tm)
    xo = _mlp(x1, p["norm_mlp"], p["wu"], p["wd"], _pick(m, (256, 128)))

    new_conv = h.reshape(nb, lt, NP)[:, lt - (CONV_WIDTH - 1):, OFF_XBC:OFF_XBC + CONV_DIM]
    state = (ko.reshape(nb, lt, N_KV_HEADS, HEAD_DIM), vo.reshape(nb, lt, N_KV_HEADS, HEAD_DIM),
             iko.reshape(nb, lt, IDX_DIM),
             hl[:, SUBLANES - 1, 0:S5_LANES].reshape(nb, S5_GROUPS, S5_STATE),
             hl[:, SUBLANES - 1, S5_LANES:].reshape(nb, S5_GROUPS, S5_STATE),
             new_conv, st.reshape(nb, SSM_HEADS, SSM_HEAD_DIM, SSM_STATE))
    return xo, state


def _layer_sample(x2, p, tab, caches, states):
    cache_k, cache_v, cache_ik, page_table, layer = caches
    s5_re, s5_im, conv0, ssm0 = states
    ns = x2.shape[0]
    ng = ns // SUBLANES
    h = _inproj(x2, p["norm_mix"], p["wp"], ns)
    qb, ko, kb, vo, vb, iqh, iql, iko, _ = _post(
        h, tab, p["qg"], p["kg"], p["bdq"], p["bdk"], 1, ns, ns, False)
    oatt = _dsa_sample(page_table, qb, iqh, iql, h, kb, vb, iko, cache_k, cache_v, cache_ik, layer)

    u4 = h.reshape(ng, SUBLANES, 1, NP)
    h0 = jnp.concatenate([s5_re.reshape(ng, SUBLANES, S5_LANES), s5_im.reshape(ng, SUBLANES, S5_LANES)], axis=2)
    os5, hl = _s5_main(u4, OFF_U // S5_WIDTH, h0, p["bmat"], p["cmat"], p["ar"], p["ai"], p["dskip"],
                       p["wglu"], 1)

    ossm, cst, st = _ssd_step(h, conv0.transpose(1, 0, 2), ssm0.reshape(ns, SSM_INNER, SSM_STATE),
                              p["cw"], p["cb"], p["dtb"], p["alog"], p["dexp"], p["nrm"], p["expand"])

    x1 = _merge(x2, h, os5.reshape(ns, S5_WIDTH), oatt, ossm, p["w1"], p["w2"], p["w3"], p["wo"], ns)
    xo = _mlp(x1, p["norm_mlp"], p["wu"], p["wd"], ns)
    state = (ko.reshape(ns, 1, N_KV_HEADS, HEAD_DIM), vo.reshape(ns, 1, N_KV_HEADS, HEAD_DIM),
             iko.reshape(ns, 1, IDX_DIM),
             hl[:, :, 0:S5_LANES].reshape(ns, S5_GROUPS, S5_STATE),
             hl[:, :, S5_LANES:].reshape(ns, S5_GROUPS, S5_STATE),
             cst.transpose(1, 0, 2), st.reshape(ns, SSM_HEADS, SSM_HEAD_DIM, SSM_STATE))
    return xo, state


_WEIGHTS =("norm_mix", "w_in", "q_norm", "k_norm", "s5_a_re", "s5_a_im", "s5_log_dt", "s5_b_re", "s5_b_im",
            "s5_c_re", "s5_c_im", "s5_d", "s5_w_glu", "conv_w", "conv_b", "dt_bias", "a_log", "ssm_d", "ssm_norm",
            "w_br_s5", "w_br_att", "w_br_ssm", "w_o", "norm_mlp", "w_up", "w_down")


def kernel(x_prompt, x_sample, cache_k, cache_v, cache_idx_k, state_s5_re, state_s5_im, state_conv, state_ssm, page_table, norm_mix, w_in, q_norm, k_norm, s5_a_re, s5_a_im, s5_log_dt, s5_b_re, s5_b_im, s5_c_re, s5_c_im, s5_d, s5_w_glu, conv_w, conv_b, dt_bias, a_log, ssm_d, ssm_norm, w_br_s5, w_br_att, w_br_ssm, w_o, norm_mlp, w_up, w_down):
    weights = dict(zip(_WEIGHTS, (norm_mix, w_in, q_norm, k_norm, s5_a_re, s5_a_im, s5_log_dt, s5_b_re, s5_b_im,
                                  s5_c_re, s5_c_im, s5_d, s5_w_glu, conv_w, conv_b, dt_bias, a_log, ssm_d, ssm_norm,
                                  w_br_s5, w_br_att, w_br_ssm, w_o, norm_mlp, w_up, w_down)))
    depth = w_in.shape[0]
    nb, lt, _ = x_prompt.shape
    ns, n_new, _ = x_sample.shape
    assert n_new == 1 and ns % SUBLANES == 0 and lt % (SUBLANES * Q_BLOCK) == 0
    past = page_table.shape[1] * PAGE_SIZE
    tab_p = _rope_table(jnp.arange(lt))
    tab_s = _rope_table(jnp.full((ns,), past, jnp.int32))
    xp = x_prompt.reshape(nb * lt, D_MODEL)
    xs = x_sample.reshape(ns, D_MODEL)
    st_p, st_s = [], []
    for l in range(depth):
        p = _layer_params({k: v[l] for k, v in weights.items()}, lt // SUBLANES)
        xp, sp = _layer_prompt(xp, p, tab_p, nb, lt)
        caches = (cache_k, cache_v, cache_idx_k, page_table, l)
        states = (state_s5_re[l], state_s5_im[l], state_conv[l], state_ssm[l])
        xs, ss = _layer_sample(xs, p, tab_s, caches, states)
        st_p.append(sp)
        st_s.append(ss)
    outs_p = [jnp.stack([s[i] for s in st_p]) for i in range(7)]
    outs_s = [jnp.stack([s[i] for s in st_s]) for i in range(7)]
    return (xp.reshape(nb, lt, D_MODEL), xs.reshape(ns, 1, D_MODEL), *outs_p, *outs_s)
```
